```python
import math
import jax, jax.numpy as jnp
from jax import lax
import numpy as np

D_MODEL = 1024
BATCH = 4
SEQ = 8192
DEPTH = 4
DEC_BATCH = 32
DEC_SEQ = 64
PAST_LEN = 1024

CHUNK = 64
N_MEM = 256
N_A_LAYERS = DEPTH // 2
N_B_LAYERS = DEPTH - N_A_LAYERS
MIX_WIDTH = D_MODEL
MEM_HEADS = 4
MEM_WIDTH = MIX_WIDTH // 4
MEM_HEAD_DIM = MEM_WIDTH // MEM_HEADS
MAIN_WIDTH = MIX_WIDTH - MEM_WIDTH
MLSTM_HEADS = 4
MLSTM_HEAD_DIM = MAIN_WIDTH // MLSTM_HEADS
DIFF_HEAD_DIM = 64
DIFF_V_DIM = 2 * DIFF_HEAD_DIM
DIFF_HEADS = MAIN_WIDTH // DIFF_V_DIM
FFN_DIM = ((8 * D_MODEL // 3 + 255) // 256) * 256
A_IN_WIDTH = 4 * MAIN_WIDTH + 2 * MLSTM_HEADS + MEM_WIDTH
B_IN_WIDTH = 2 * DIFF_HEADS * DIFF_HEAD_DIM + MEM_WIDTH
KV_WIDTH = 2 * DIFF_HEADS * DIFF_HEAD_DIM + DIFF_HEADS * DIFF_V_DIM
Q_BLOCK = 128
EPS = 1e-6
F32 = jnp.float32

kernel_name = "yoco_mlstm_diffattn_streaming_step"


def rmsnorm(x, g):
    xf = x.astype(F32)
    y = xf * lax.rsqrt(jnp.mean(xf * xf, axis=-1, keepdims=True) + EPS)
    return (y * g.astype(F32)).astype(x.dtype)


def swiglu(x, w_up, w_down):
    g, u = jnp.split(x @ w_up, 2, axis=-1)
    return (jax.nn.silu(g) * u) @ w_down


def alibi_slopes():
    return jnp.exp2(-8.0 * jnp.arange(1, DIFF_HEADS + 1, dtype=F32) / DIFF_HEADS)


def mlstm_chunk(carry, inp):
    C, n, m = carry
    q, k, v, ig, lf = inp
    L = q.shape[2]
    b = jnp.cumsum(lf, axis=-1)
    causal = jnp.tril(jnp.ones((L, L), bool))
    d_log = jnp.where(causal, b[..., :, None] - b[..., None, :] + ig[..., None, :], -jnp.inf)
    g = b + m[..., None]
    m_t = jnp.maximum(g, jnp.max(d_log, axis=-1))
    w_inter = jnp.exp(g - m_t)
    s = jnp.einsum('bhtd,bhsd->bhts', q, k) * jnp.exp(d_log - m_t[..., None])
    num = w_inter[..., None] * jnp.einsum('bhed,bhtd->bhte', C, q) + jnp.einsum('bhts,bhse->bhte', s, v)
    nq = w_inter * jnp.einsum('bhd,bhtd->bht', n, q) + jnp.sum(s, axis=-1)
    h = num / jnp.maximum(jnp.abs(nq), jnp.exp(-m_t))[..., None]
    m_new = m_t[..., -1]
    w_end = jnp.exp(b[..., -1:] - b + ig - m_new[..., None])
    decay = jnp.exp(b[..., -1] + m - m_new)
    C_new = decay[..., None, None] * C + jnp.einsum('bhs,bhse,bhsd->bhed', w_end, v, k)
    n_new = decay[..., None] * n + jnp.einsum('bhs,bhsd->bhd', w_end, k)
    return (C_new, n_new, m_new), h


def mlstm_scan(q, k, v, ig, lf, state, chunk):
    B, T = q.shape[:2]
    nc = T // chunk

    def blk(a):
        a = a.astype(F32).reshape((B, nc, chunk) + a.shape[2:])
        return jnp.moveaxis(jnp.moveaxis(a, 1, 0), 3, 2)

    state = tuple(s.astype(F32) for s in state)
    state, h = lax.scan(mlstm_chunk, state, (blk(q), blk(k), blk(v), blk(ig), blk(lf)))
    h = jnp.moveaxis(jnp.moveaxis(h, 2, 3), 0, 1).reshape(B, T, MLSTM_HEADS, MLSTM_HEAD_DIM)
    return h, state


def mlstm_mixer(p, b_gate, g_head, state, chunk):
    B, T, _ = p.shape
    q, k, v, o, gates = jnp.split(p, [MAIN_WIDTH, 2 * MAIN_WIDTH, 3 * MAIN_WIDTH, 4 * MAIN_WIDTH], axis=-1)
    shp = (B, T, MLSTM_HEADS, MLSTM_HEAD_DIM)
    gates = gates.astype(F32) + b_gate.astype(F32)
    ig = gates[..., :MLSTM_HEADS]
    lf = jax.nn.log_sigmoid(gates[..., MLSTM_HEADS:])
    h, state = mlstm_scan(q.reshape(shp), k.reshape(shp) * (MLSTM_HEAD_DIM ** -0.5), v.reshape(shp),
                          ig, lf, state, chunk)
    h = rmsnorm(h, g_head) * jax.nn.sigmoid(o.reshape(shp).astype(F32))
    return h.reshape(B, T, MAIN_WIDTH).astype(p.dtype), state


def mem_attention(q, mk, mv):
    s = jnp.einsum('bthd,bmhd->bhtm', q, mk.astype(q.dtype)).astype(F32) * (MEM_HEAD_DIM ** -0.5)
    p = jax.nn.softmax(s, axis=-1).astype(q.dtype)
    return jnp.einsum('bhtm,bmhd->bthd', p, mv.astype(q.dtype))


def diff_attn_block(q1, q2, k1, k2, v, q_pos, k_pos, lam, slopes):
    dist = jnp.abs(q_pos[:, None] - k_pos[None, :]).astype(F32)
    visible = (k_pos[None, :] // CHUNK) <= (q_pos[:, None] // CHUNK)
    bias = jnp.where(visible[None], -slopes[:, None, None] * dist[None], -jnp.inf)
    scale = DIFF_HEAD_DIM ** -0.5

    def smax(q, k):
        s = jnp.einsum('bthd,bshd->bhts', q, k).astype(F32) * scale + bias
        return jax.nn.softmax(s, axis=-1)

    a = smax(q1, k1) - lam * smax(q2, k2)
    return jnp.einsum('bhts,bshe->bthe', a.astype(v.dtype), v)


def diff_attention(q1, q2, k1, k2, v, q_start, lam, slopes):
    B, T = q1.shape[:2]
    k_pos = jnp.arange(k1.shape[1])
    if T > Q_BLOCK and T % Q_BLOCK == 0:
        nb = T // Q_BLOCK
        blk = lambda a: a.reshape((B, nb, Q_BLOCK) + a.shape[2:]).swapaxes(0, 1)

        def body(args):
            i, a1, a2 = args
            q_pos = q_start + i * Q_BLOCK + jnp.arange(Q_BLOCK)
            return diff_attn_block(a1, a2, k1, k2, v, q_pos, k_pos, lam, slopes)

        out = lax.map(body, (jnp.arange(nb), blk(q1), blk(q2)))
        return out.swapaxes(0, 1).reshape(B, T, DIFF_HEADS, DIFF_V_DIM)
    return diff_attn_block(q1, q2, k1, k2, v, q_start + jnp.arange(T), k_pos, lam, slopes)


def shared_kv(x, g_kv, w_kv):
    B, T, _ = x.shape
    h = rmsnorm(x, g_kv) @ w_kv
    nk = 2 * DIFF_HEADS * DIFF_HEAD_DIM
    k = h[..., :nk].reshape(B, T, 2 * DIFF_HEADS, DIFF_HEAD_DIM)
    v = h[..., nk:].reshape(B, T, DIFF_HEADS, DIFF_V_DIM)
    return k, v


def mem_kv(mem, w):
    B = mem.shape[0]
    k, v = jnp.split(mem @ w, 2, axis=-1)
    shp = (B, N_MEM, MEM_HEADS, MEM_HEAD_DIM)
    return k.reshape(shp), v.reshape(shp)


def trunk(x, mem_k, mem_v, C0, n0, m0, past_k, past_v, q_start, chunk,
          w_up, w_down, norm_g, w_in_a, b_gates, g_head_a, w_in_b, lambdas, g_sub_b, g_kv, w_kv, w_out):
    B, T, _ = x.shape
    slopes = alibi_slopes()
    Cs, ns, ms = [], [], []
    new_k = new_v = k_all = v_all = None
    for l in range(DEPTH):
        x = x + 0.5 * rmsnorm(swiglu(rmsnorm(x, norm_g[l, 0]), w_up[l, 0], w_down[l, 0]), norm_g[l, 1])
        h = rmsnorm(x, norm_g[l, 2])
        if l < N_A_LAYERS:
            p = h @ w_in_a[l]
            main_p, mq = p[..., :-MEM_WIDTH], p[..., -MEM_WIDTH:]
            main_out, (C, n, m) = mlstm_mixer(main_p, b_gates[l], g_head_a[l], (C0[l], n0[l], m0[l]), chunk)
            Cs.append(C); ns.append(n); ms.append(m)
        else:
            lb = l - N_A_LAYERS
            p = h @ w_in_b[lb]
            main_p, mq = p[..., :-MEM_WIDTH], p[..., -MEM_WIDTH:]
            nq = DIFF_HEADS * DIFF_HEAD_DIM
            q1 = main_p[..., :nq].reshape(B, T, DIFF_HEADS, DIFF_HEAD_DIM)
            q2 = main_p[..., nq:].reshape(B, T, DIFF_HEADS, DIFF_HEAD_DIM)
            lam_init = 0.8 - 0.6 * math.exp(-0.3 * l)
            lp = lambdas[lb].astype(F32)
            lam = jnp.exp(jnp.sum(lp[0] * lp[1])) - jnp.exp(jnp.sum(lp[2] * lp[3])) + lam_init
            o = diff_attention(q1, q2, k_all[:, :, :DIFF_HEADS], k_all[:, :, DIFF_HEADS:], v_all,
                               q_start, lam, slopes)
            main_out = (rmsnorm(o, g_sub_b[lb]) * (1.0 - lam_init)).reshape(B, T, MAIN_WIDTH).astype(x.dtype)
        mem_out = mem_attention(mq.reshape(B, T, MEM_HEADS, MEM_HEAD_DIM), mem_k[l], mem_v[l])
        mix = jnp.concatenate([main_out, mem_out.reshape(B, T, MEM_WIDTH).astype(x.dtype)], axis=-1)
        x = x + rmsnorm(mix @ w_out[l], norm_g[l, 3])
        x = x + 0.5 * rmsnorm(swiglu(rmsnorm(x, norm_g[l, 4]), w_up[l, 1], w_down[l, 1]), norm_g[l, 5])
        if l == N_A_LAYERS - 1:
            new_k, new_v = shared_kv(x, g_kv, w_kv)
            if past_k is None:
                k_all, v_all = new_k, new_v
            else:
                k_all = jnp.concatenate([past_k.astype(new_k.dtype), new_k], axis=1)
                v_all = jnp.concatenate([past_v.astype(new_v.dtype), new_v], axis=1)
    return x, jnp.stack(Cs), jnp.stack(ns), jnp.stack(ms), new_k, new_v


def setup_inputs(seed: int = 0) -> dict:
    key = jax.random.key(seed)
    ks = jax.random.split(key, 24)
    nrm = lambda k, shape, s=1.0: jax.random.normal(k, shape, F32) * s
    H, dh = MLSTM_HEADS, MLSTM_HEAD_DIM
    b_in = nrm(ks[14], (N_A_LAYERS, H), 0.1)
    b_f = jnp.linspace(3.0, 6.0, H, dtype=F32)[None] + nrm(ks[15], (N_A_LAYERS, H), 0.1)
    return {
        "x_prompt": nrm(ks[0], (BATCH, SEQ, D_MODEL)),
        "x_sample": nrm(ks[1], (DEC_BATCH, DEC_SEQ, D_MODEL)),
        "mem_prompt": nrm(ks[2], (BATCH, N_MEM, D_MODEL)),
        "cache_mem_k": nrm(ks[3], (DEPTH, DEC_BATCH, N_MEM, MEM_HEADS, MEM_HEAD_DIM)),
        "cache_mem_v": nrm(ks[4], (DEPTH, DEC_BATCH, N_MEM, MEM_HEADS, MEM_HEAD_DIM)),
        "state_C": nrm(ks[5], (N_A_LAYERS, DEC_BATCH, H, dh, dh), 0.05),
        "state_n": nrm(ks[6], (N_A_LAYERS, DEC_BATCH, H, dh), 0.05),
        "state_m": 1.0 + nrm(ks[7], (N_A_LAYERS, DEC_BATCH, H), 0.5),
        "cache_k": nrm(ks[8], (DEC_BATCH, PAST_LEN, 2 * DIFF_HEADS, DIFF_HEAD_DIM)),
        "cache_v": nrm(ks[9], (DEC_BATCH, PAST_LEN, DIFF_HEADS, DIFF_V_DIM)),
        "w_up": nrm(ks[10], (DEPTH, 2, D_MODEL, 2 * FFN_DIM), D_MODEL ** -0.5),
        "w_down": nrm(ks[11], (DEPTH, 2, FFN_DIM, D_MODEL), FFN_DIM ** -0.5),
        "norm_g": 1.0 + nrm(ks[12], (DEPTH, 6, D_MODEL), 0.02),
        "w_in_a": nrm(ks[13], (N_A_LAYERS, D_MODEL, A_IN_WIDTH), D_MODEL ** -0.5),
        "b_gates": jnp.concatenate([b_in, b_f], axis=-1),
        "g_head_a": 1.0 + nrm(ks[16], (N_A_LAYERS, H, dh), 0.02),
        "w_in_b": nrm(ks[17], (N_B_LAYERS, D_MODEL, B_IN_WIDTH), D_MODEL ** -0.5),
        "lambdas": nrm(ks[18], (N_B_LAYERS, 4, DIFF_HEAD_DIM), 0.1),
        "g_sub_b": 1.0 + nrm(ks[19], (N_B_LAYERS, DIFF_V_DIM), 0.02),
        "g_kv": 1.0 + nrm(ks[20], (D_MODEL,), 0.02),
        "w_kv": nrm(ks[21], (D_MODEL, KV_WIDTH), D_MODEL ** -0.5),
        "w_mem_kv": nrm(ks[22], (DEPTH, D_MODEL, 2 * MEM_WIDTH), D_MODEL ** -0.5),
        "w_out": nrm(ks[23], (DEPTH, MIX_WIDTH, D_MODEL), MIX_WIDTH ** -0.5),
    }


def reference(x_prompt, x_sample, mem_prompt, cache_mem_k, cache_mem_v, state_C, state_n, state_m,
              cache_k, cache_v, w_up, w_down, norm_g, w_in_a, b_gates, g_head_a, w_in_b, lambdas,
              g_sub_b, g_kv, w_kv, w_mem_kv, w_out):
    weights = (w_up, w_down, norm_g, w_in_a, b_gates, g_head_a, w_in_b, lambdas, g_sub_b, g_kv, w_kv, w_out)
    mkv = [mem_kv(mem_prompt, w_mem_kv[l]) for l in range(DEPTH)]
    mem_k_prompt = jnp.stack([a for a, _ in mkv])
    mem_v_prompt = jnp.stack([b for _, b in mkv])
    Bp = x_prompt.shape[0]
    zC = jnp.zeros((N_A_LAYERS, Bp, MLSTM_HEADS, MLSTM_HEAD_DIM, MLSTM_HEAD_DIM), F32)
    zn = jnp.zeros((N_A_LAYERS, Bp, MLSTM_HEADS, MLSTM_HEAD_DIM), F32)
    zm = jnp.zeros((N_A_LAYERS, Bp, MLSTM_HEADS), F32)
    y_prompt, C_prompt, n_prompt, m_prompt, k_prompt, v_prompt = trunk(
        x_prompt, mem_k_prompt, mem_v_prompt, zC, zn, zm, None, None, 0, CHUNK, *weights)
    y_sample, C_sample, n_sample, m_sample, k_sample, v_sample = trunk(
        x_sample, cache_mem_k, cache_mem_v, state_C, state_n, state_m, cache_k, cache_v,
        PAST_LEN, x_sample.shape[1], *weights)
    return (y_prompt, y_sample, mem_k_prompt, mem_v_prompt, C_prompt, n_prompt, m_prompt,
            k_prompt, v_prompt, C_sample, n_sample, m_sample, k_sample, v_sample)
```

```python
import functools
import math

import jax
import jax.numpy as jnp
from jax import lax
from jax.experimental import pallas as pl
from jax.experimental.pallas import tpu as pltpu

F32 = jnp.float32
BF16 = jnp.bfloat16

D_MODEL = 1024
DEPTH = 4
CHUNK = 64
N_MEM = 256
N_A_LAYERS = DEPTH // 2
MEM_HEADS = 4
MEM_WIDTH = 256
MEM_HEAD_DIM = 64
MAIN_WIDTH = 768
MLSTM_HEADS = 4
MLSTM_HEAD_DIM = 192
DIFF_HEAD_DIM = 64
DIFF_V_DIM = 128
DIFF_HEADS = 6
FFN_DIM = 2816
EPS = 1e-6

LANES = 128
HEAD_PAD = 256
N_COL = MLSTM_HEAD_DIM
A_WIDTH = MLSTM_HEADS * HEAD_PAD
QZ_WIDTH = 2 * DIFF_HEADS * LANES
MQZ_WIDTH = MEM_HEADS * LANES
LOG2E = math.log2(math.e)
VMEM_LIMIT = 56 * 1024 * 1024

TOKEN_TILE = 512
MLSTM_TILE = 512
ATTN_TILE = 512
MEM_TILE = 512


def _params(*sem):
    return pltpu.CompilerParams(dimension_semantics=sem, vmem_limit_bytes=VMEM_LIMIT)


def _rms(x, g):
    return x * lax.rsqrt(jnp.mean(x * x, axis=-1, keepdims=True) + EPS) * g


def _log_sigmoid(x):
    return jnp.minimum(x, 0.0) - jnp.log1p(jnp.exp(-jnp.abs(x)))


def _dot(a, b):
    return jnp.dot(a, b, preferred_element_type=F32)


def _dot_nt(a, b):
    return lax.dot_general(a, b, (((1,), (1,)), ((), ())), preferred_element_type=F32)


def _resident(shape, index_map):
    return pl.BlockSpec(shape, index_map, pipeline_mode=pl.Buffered(1))


def _ffn_kernel(x_ref, g0_ref, wg_ref, wu_ref, wd_ref, g1_ref, o_ref):
    x = x_ref[...]
    xb = _rms(x, g0_ref[...]).astype(BF16)
    hg = _dot(xb, wg_ref[...])
    hu = _dot(xb, wu_ref[...])
    a = (hg * jax.nn.sigmoid(hg) * hu).astype(BF16)
    y = _dot(a, wd_ref[...])
    o_ref[...] = x + 0.5 * _rms(y, g1_ref[...])


def _ffn(x, g0, w_up, w_down, g1, l, k):
    n = x.shape[0]
    tm = min(TOKEN_TILE, n)
    return pl.pallas_call(
        _ffn_kernel,
        out_shape=jax.ShapeDtypeStruct((n, D_MODEL), F32),
        grid=(n // tm,),
        in_specs=[
            pl.BlockSpec((tm, D_MODEL), lambda i: (i, 0)),
            pl.BlockSpec((1, D_MODEL), lambda i: (0, 0)),
            _resident((None, None, D_MODEL, FFN_DIM), lambda i: (l, k, 0, 0)),
            _resident((None, None, D_MODEL, FFN_DIM), lambda i: (l, k, 0, 1)),
            _resident((None, None, FFN_DIM, D_MODEL), lambda i: (l, k, 0, 0)),
            pl.BlockSpec((1, D_MODEL), lambda i: (0, 0)),
        ],
        out_specs=pl.BlockSpec((tm, D_MODEL), lambda i: (i, 0)),
        compiler_params=_params("arbitrary"),
        name="ffn",
    )(x, g0, w_up, w_up, w_down, g1)


def _inproj_a_kernel(x_ref, g_ref, wrow_ref, wkt_ref, wgt_ref, brow_ref, bcol_ref,
                     q_ref, kt_ref, v_ref, o_ref, mq_ref, gcol_ref, grow_ref):
    xb = _rms(x_ref[...], g_ref[...]).astype(BF16)
    y = _dot(xb, wrow_ref[...])
    q_ref[...] = y[:, 0:A_WIDTH].astype(BF16)
    v = y[:, A_WIDTH:2 * A_WIDTH]
    lane = lax.broadcasted_iota(jnp.int32, v.shape, 1)
    v_ref[...] = jnp.where(lane % HEAD_PAD == N_COL, 1.0, v).astype(BF16)
    o_ref[...] = y[:, 2 * A_WIDTH:3 * A_WIDTH]
    mq_ref[...] = (y[:, 3 * A_WIDTH:3 * A_WIDTH + MQZ_WIDTH] * (MEM_HEAD_DIM ** -0.5 * LOG2E)).astype(BF16)
    g = y[:, 3 * A_WIDTH + MQZ_WIDTH:] + brow_ref[...]
    glane = lax.broadcasted_iota(jnp.int32, g.shape, 1)
    gcol_ref[...] = jnp.where(glane < MLSTM_HEADS, g, _log_sigmoid(g))
    kt = _dot_nt(wkt_ref[...], xb)
    kt_ref[...] = (kt * (MLSTM_HEAD_DIM ** -0.5)).astype(BF16)
    gt = _dot_nt(wgt_ref[...], xb) + bcol_ref[...]
    grow_idx = lax.broadcasted_iota(jnp.int32, gt.shape, 0)
    grow_ref[...] = jnp.where(grow_idx < MLSTM_HEADS, gt, _log_sigmoid(gt))


def _inproj_a(x, g, wrow, wkt, wgt, brow, bcol, l, groups):
    n = x.shape[0]
    tg = n // groups
    tm = min(TOKEN_TILE, tg)
    per = tg // tm
    wcols = wrow.shape[2]
    outs = (
        jax.ShapeDtypeStruct((n, A_WIDTH), BF16),
        jax.ShapeDtypeStruct((groups, A_WIDTH, tg), BF16),
        jax.ShapeDtypeStruct((n, A_WIDTH), BF16),
        jax.ShapeDtypeStruct((n, A_WIDTH), F32),
        jax.ShapeDtypeStruct((n, MQZ_WIDTH), BF16),
        jax.ShapeDtypeStruct((n, LANES), F32),
        jax.ShapeDtypeStruct((groups, 8, tg), F32),
    )
    row = lambda w: pl.BlockSpec((tm, w), lambda i: (i, 0))
    return pl.pallas_call(
        _inproj_a_kernel,
        out_shape=outs,
        grid=(n // tm,),
        in_specs=[
            row(D_MODEL),
            pl.BlockSpec((1, D_MODEL), lambda i: (0, 0)),
            _resident((None, D_MODEL, wcols), lambda i: (l, 0, 0)),
            _resident((None, A_WIDTH, D_MODEL), lambda i: (l, 0, 0)),
            _resident((None, 8, D_MODEL), lambda i: (l, 0, 0)),
            pl.BlockSpec((None, 1, LANES), lambda i: (l, 0, 0)),
            pl.BlockSpec((None, 8, 1), lambda i: (l, 0, 0)),
        ],
        out_specs=(
            row(A_WIDTH),
            pl.BlockSpec((None, A_WIDTH, tm), lambda i: (i // per, 0, i % per)),
            row(A_WIDTH),
            row(A_WIDTH),
            row(MQZ_WIDTH),
            row(LANES),
            pl.BlockSpec((None, 8, tm), lambda i: (i // per, 0, i % per)),
        ),
        compiler_params=_params("arbitrary"),
        name="inproj_a",
    )(x, g, wrow, wkt, wgt, brow, bcol)


def _inproj_b_kernel(x_ref, g_ref, w_ref, qz_ref, mq_ref):
    xb = _rms(x_ref[...], g_ref[...]).astype(BF16)
    y = _dot(xb, w_ref[...])
    qz_ref[...] = (y[:, :QZ_WIDTH] * (DIFF_HEAD_DIM ** -0.5 * LOG2E)).astype(BF16)
    mq_ref[...] = (y[:, QZ_WIDTH:] * (MEM_HEAD_DIM ** -0.5 * LOG2E)).astype(BF16)


def _inproj_b(x, g, w, lb):
    n = x.shape[0]
    tm = min(TOKEN_TILE, n)
    return pl.pallas_call(
        _inproj_b_kernel,
        out_shape=(jax.ShapeDtypeStruct((n, QZ_WIDTH), BF16), jax.ShapeDtypeStruct((n, MQZ_WIDTH), BF16)),
        grid=(n // tm,),
        in_specs=[
            pl.BlockSpec((tm, D_MODEL), lambda i: (i, 0)),
            pl.BlockSpec((1, D_MODEL), lambda i: (0, 0)),
            _resident((None, D_MODEL, QZ_WIDTH + MQZ_WIDTH), lambda i: (lb, 0, 0)),
        ],
        out_specs=(pl.BlockSpec((tm, QZ_WIDTH), lambda i: (i, 0)), pl.BlockSpec((tm, MQZ_WIDTH), lambda i: (i, 0))),
        compiler_params=_params("arbitrary"),
        name="inproj_b",
    )(x, g, w)


def _kvproj_kernel(x_ref, g_ref, w_ref, k_ref, v_ref, kb_ref, vb_ref):
    xb = _rms(x_ref[...], g_ref[...]).astype(BF16)
    y = _dot(xb, w_ref[...])
    k = y[:, :MAIN_WIDTH]
    v = y[:, MAIN_WIDTH:]
    k_ref[...] = k
    v_ref[...] = v
    kb_ref[...] = k.astype(BF16)
    vb_ref[...] = v.astype(BF16)


def _kvproj(x, g, w):
    n = x.shape[0]
    tm = min(TOKEN_TILE, n)
    blk = pl.BlockSpec((tm, MAIN_WIDTH), lambda i: (i, 0))
    return pl.pallas_call(
        _kvproj_kernel,
        out_shape=(jax.ShapeDtypeStruct((n, MAIN_WIDTH), F32), jax.ShapeDtypeStruct((n, MAIN_WIDTH), F32),
                   jax.ShapeDtypeStruct((n, MAIN_WIDTH), BF16), jax.ShapeDtypeStruct((n, MAIN_WIDTH), BF16)),
        grid=(n // tm,),
        in_specs=[
            pl.BlockSpec((tm, D_MODEL), lambda i: (i, 0)),
            pl.BlockSpec((1, D_MODEL), lambda i: (0, 0)),
            _resident((D_MODEL, 2 * MAIN_WIDTH), lambda i: (0, 0)),
        ],
        out_specs=(blk, blk, blk, blk),
        compiler_params=_params("arbitrary"),
        name="kvproj",
    )(x, g, w)


def _memkv_kernel(m_ref, w_ref, k_ref, v_ref):
    y = _dot(m_ref[...].astype(BF16), w_ref[...])
    k_ref[...] = y[:, :MEM_WIDTH]
    v_ref[...] = y[:, MEM_WIDTH:]


def _memkv(mem, w):
    n = mem.shape[0]
    out = jax.ShapeDtypeStruct((DEPTH, n, MEM_WIDTH), F32)
    blk = pl.BlockSpec((None, n, MEM_WIDTH), lambda l: (l, 0, 0))
    return pl.pallas_call(
        _memkv_kernel,
        out_shape=(out, out),
        grid=(DEPTH,),
        in_specs=[pl.BlockSpec((n, D_MODEL), lambda l: (0, 0)),
                  pl.BlockSpec((None, D_MODEL, 2 * MEM_WIDTH), lambda l: (l, 0, 0))],
        out_specs=(blk, blk),
        compiler_params=_params("arbitrary"),
        name="memkv",
    )(mem, w)


def _split3(x):
    hi = x.astype(BF16)
    r = x - hi.astype(F32)
    mid = r.astype(BF16)
    lo = (r - mid.astype(F32)).astype(BF16)
    return hi, mid, lo


def _mlstm_kernel(q_ref, kt_ref, v_ref, o_ref, gcol_ref, grow_ref, gh_ref, ct0_ref, m0_ref,
                  h_ref, ctn_ref, mn_ref, ct_s, m_s, tri_s, *, sequential, n_chunks):
    t = pl.program_id(1)
    tb = n_chunks * CHUNK

    @pl.when((pl.program_id(0) == 0) & (t == 0))
    def _init_tri():
        r = lax.broadcasted_iota(jnp.int32, (tb, tb), 0)
        c = lax.broadcasted_iota(jnp.int32, (tb, tb), 1)
        tri_s[...] = jnp.where((r // CHUNK == c // CHUNK) & (c <= r), 1.0, 0.0).astype(BF16)

    @pl.when((t == 0) | (not sequential))
    def _init_state():
        ct_s[...] = ct0_ref[...]
        m_s[...] = m0_ref[...]

    tri = tri_s[...]
    gcol = gcol_ref[...]
    grow = grow_ref[...]
    bcol_all = sum(_dot(tri, part) for part in _split3(gcol))
    brow_all = sum(_dot_nt(part, tri) for part in _split3(grow))

    row_i = lax.broadcasted_iota(jnp.int32, (CHUNK, LANES), 0)
    lane_i = lax.broadcasted_iota(jnp.int32, (CHUNK, LANES), 1)
    lane_row = lax.broadcasted_iota(jnp.int32, (1, LANES), 1)
    pad_lane = lax.broadcasted_iota(jnp.int32, (CHUNK, HEAD_PAD), 1)

    for c in range(n_chunks):
        slot = 0 if sequential else c
        r0 = c * CHUNK
        p0 = (c // 2) * LANES
        half = c % 2
        in_chunk = lane_i // CHUNK == half
        vis = in_chunk & (lane_i % CHUNK <= row_i)
        in_chunk_row = lane_row // CHUNK == half
        for h in range(MLSTM_HEADS):
            hs = slice(h * HEAD_PAD, (h + 1) * HEAD_PAD)
            fl = MLSTM_HEADS + h
            bcol = bcol_all[r0:r0 + CHUNK, fl:fl + 1]
            brow = brow_all[fl:fl + 1, p0:p0 + LANES]
            igrow = grow[h:h + 1, p0:p0 + LANES]
            m_prev = m_s[slot, h:h + 1, 0:1]
            dlog = jnp.where(vis, bcol - brow + igrow, -jnp.inf)
            a = jnp.max(dlog, axis=1, keepdims=True)
            mt = jnp.maximum(bcol + m_prev, a)
            e = jnp.exp(dlog - mt)
            w_inter = jnp.exp(bcol + m_prev - mt)
            qh = q_ref[r0:r0 + CHUNK, hs]
            kth = kt_ref[hs, p0:p0 + LANES]
            vh = v_ref[p0:p0 + LANES, hs]
            s = (_dot(qh, kth) * e).astype(BF16)
            ct = ct_s[slot, h]
            num = w_inter * _dot(qh, ct.astype(BF16)) + _dot(s, vh)
            nq = num[:, N_COL:N_COL + 1]
            den = jnp.maximum(jnp.abs(nq), jnp.exp(-mt))
            hh = jnp.where(pad_lane < MLSTM_HEAD_DIM, num / den, 0.0)
            ms = jnp.sum(hh * hh, axis=1, keepdims=True) * (1.0 / MLSTM_HEAD_DIM)
            hn = hh * lax.rsqrt(ms + EPS) * gh_ref[:, hs]
            h_ref[r0:r0 + CHUNK, hs] = (hn * jax.nn.sigmoid(o_ref[r0:r0 + CHUNK, hs])).astype(BF16)
            last = half * CHUNK + CHUNK - 1
            b_last = brow[:, last:last + 1]
            m_new = mt[CHUNK - 1:CHUNK, :]
            w_end = jnp.where(in_chunk_row, jnp.exp(b_last - brow + igrow - m_new), 0.0)
            decay = jnp.exp(b_last + m_prev - m_new)
            kw = (kth.astype(F32) * w_end).astype(BF16)
            ct_s[slot, h] = decay * ct + _dot(kw, vh)
            m_s[slot, h:h + 1, :] = jnp.broadcast_to(m_new, (1, LANES))

    @pl.when((t == pl.num_programs(1) - 1) | (not sequential))
    def _fin():
        ctn_ref[...] = ct_s[...]
        mn_ref[...] = m_s[...]


def _mlstm(q, kt, v, o, gcol, grow, ghead, ct0, m0, l, groups, sequential):
    n = q.shape[0]
    tg = n // groups
    tb = min(MLSTM_TILE, tg)
    n_chunks = tb // CHUNK
    per = tg // tb
    slots = 1 if sequential else n_chunks
    streams = ct0.shape[1]
    sblk = (lambda g, t: g) if sequential else (lambda g, t: g * per + t)
    row = lambda w: pl.BlockSpec((tb, w), lambda g, t: (g * per + t, 0))
    kernel = functools.partial(_mlstm_kernel, sequential=sequential, n_chunks=n_chunks)
    return pl.pallas_call(
        kernel,
        out_shape=(jax.ShapeDtypeStruct((n, A_WIDTH), BF16),
                   jax.ShapeDtypeStruct((streams, MLSTM_HEADS, HEAD_PAD, HEAD_PAD), F32),
                   jax.ShapeDtypeStruct((streams, 8, LANES), F32)),
        grid=(groups, per),
        in_specs=[
            row(A_WIDTH),
            pl.BlockSpec((None, A_WIDTH, tb), lambda g, t: (g, 0, t)),
            row(A_WIDTH),
            row(A_WIDTH),
            row(LANES),
            pl.BlockSpec((None, 8, tb), lambda g, t: (g, 0, t)),
            pl.BlockSpec((None, 1, A_WIDTH), lambda g, t: (l, 0, 0)),
            pl.BlockSpec((None, slots, MLSTM_HEADS, HEAD_PAD, HEAD_PAD), lambda g, t: (l, sblk(g, t), 0, 0, 0)),
            pl.BlockSpec((None, slots, 8, LANES), lambda g, t: (l, sblk(g, t), 0, 0)),
        ],
        out_specs=(
            row(A_WIDTH),
            pl.BlockSpec((slots, MLSTM_HEADS, HEAD_PAD, HEAD_PAD), lambda g, t: (sblk(g, t), 0, 0, 0)),
            pl.BlockSpec((slots, 8, LANES), lambda g, t: (sblk(g, t), 0, 0)),
        ),
        scratch_shapes=[
            pltpu.VMEM((slots, MLSTM_HEADS, HEAD_PAD, HEAD_PAD), F32),
            pltpu.VMEM((slots, 8, LANES), F32),
            pltpu.VMEM((tb, tb), BF16),
        ],
        compiler_params=_params("arbitrary", "arbitrary"),
        name="mlstm",
    )(q, kt, v, o, gcol, grow, ghead, ct0, m0)


_SLOPES2 = tuple(LOG2E * 2.0 ** (-8.0 * (h + 1) / DIFF_HEADS) for h in range(DIFF_HEADS))


def _lam(lam_ref, lam_init):
    lp = lam_ref[...]
    s1 = jnp.sum(lp[0:1] * lp[1:2], axis=1, keepdims=True)
    s2 = jnp.sum(lp[2:3] * lp[3:4], axis=1, keepdims=True)
    return jnp.exp(s1) - jnp.exp(s2) + lam_init


def _vext(v_tile):
    return jnp.concatenate([v_tile, jnp.ones_like(v_tile)], axis=1)


def _finish_head(acc1, acc2, lam, gsub, lam_init):
    o = acc1[:, :DIFF_V_DIM] / acc1[:, DIFF_V_DIM:] - lam * (acc2[:, :DIFF_V_DIM] / acc2[:, DIFF_V_DIM:])
    return (_rms(o, gsub) * (1.0 - lam_init)).astype(BF16)


def _attn_prompt_kernel(qz_ref, k_ref, v_ref, lam_ref, gsub_ref, o_ref, acc_s, m_s, *, lam_init, tq):
    i = pl.program_id(1)
    j = pl.program_id(2)

    @pl.when(j == 0)
    def _init():
        acc_s[...] = jnp.zeros_like(acc_s)
        m_s[...] = jnp.full_like(m_s, -jnp.inf)

    def step(bias_fn):
        for a in range(2):
            for h in range(DIFF_HEADS):
                idx = a * DIFF_HEADS + h
                qt = qz_ref[:, idx * LANES:(idx + 1) * LANES]
                kt_i = a * (DIFF_HEADS // 2) + h // 2
                kt = k_ref[:, kt_i * LANES:(kt_i + 1) * LANES].astype(BF16)
                s = bias_fn(_dot_nt(qt, kt), h)
                m_prev = m_s[idx]
                m_new = jnp.maximum(m_prev, jnp.max(s, axis=1, keepdims=True))
                p = jnp.exp2(s - m_new[:, 0:1]).astype(BF16)
                alpha = jnp.exp2(m_prev - m_new)
                vt = _vext(v_ref[:, h * LANES:(h + 1) * LANES].astype(BF16))
                acc_s[idx] = jnp.concatenate([alpha, alpha], axis=1) * acc_s[idx] + _dot(p, vt)
                m_s[idx] = m_new

    @pl.when(j < i)
    def _past():
        col = lax.broadcasted_iota(jnp.int32, (1, tq), 1).astype(F32)
        off = ((j - i) * tq).astype(F32)
        step(lambda s, h: s + _SLOPES2[h] * (col + off))

    @pl.when(j == i)
    def _diag():
        r = lax.broadcasted_iota(jnp.int32, (tq, tq), 0)
        c = lax.broadcasted_iota(jnp.int32, (tq, tq), 1)
        vis = c // CHUNK <= r // CHUNK
        g = (r - jnp.abs(r - c)).astype(F32)
        step(lambda s, h: jnp.where(vis, s + _SLOPES2[h] * g, -jnp.inf))
        lam = _lam(lam_ref, lam_init)
        for h in range(DIFF_HEADS):
            o_ref[:, h * LANES:(h + 1) * LANES] = _finish_head(
                acc_s[h], acc_s[DIFF_HEADS + h], lam, gsub_ref[...], lam_init)


def _attn_prompt(qz, kb, vb, lambdas, gsub, lam_init, groups):
    n = qz.shape[0]
    tg = n // groups
    tq = min(ATTN_TILE, tg)
    nq = tg // tq
    kernel = functools.partial(_attn_prompt_kernel, lam_init=lam_init, tq=tq)
    kv_spec = pl.BlockSpec((tq, MAIN_WIDTH), lambda b, i, j: (b * nq + jnp.minimum(i, j), 0))
    return pl.pallas_call(
        kernel,
        out_shape=jax.ShapeDtypeStruct((n, MAIN_WIDTH), BF16),
        grid=(groups, nq, nq),
        in_specs=[
            pl.BlockSpec((tq, QZ_WIDTH), lambda b, i, j: (b * nq + i, 0)),
            kv_spec,
            kv_spec,
            pl.BlockSpec((4, DIFF_HEAD_DIM), lambda b, i, j: (0, 0)),
            pl.BlockSpec((1, DIFF_V_DIM), lambda b, i, j: (0, 0)),
        ],
        out_specs=pl.BlockSpec((tq, MAIN_WIDTH), lambda b, i, j: (b * nq + i, 0)),
        scratch_shapes=[
            pltpu.VMEM((2 * DIFF_HEADS, tq, 2 * DIFF_V_DIM), F32),
            pltpu.VMEM((2 * DIFF_HEADS, tq, LANES), F32),
        ],
        compiler_params=_params("arbitrary", "arbitrary", "arbitrary"),
        name="attn_prompt",
    )(qz, kb, vb, lambdas, gsub)


def _attn_sample_kernel(qz_ref, ck_ref, cv_ref, nk_ref, nv_ref, lam_ref, gsub_ref, o_ref, *, lam_init, past):
    tq = qz_ref.shape[0]

    def geometry(width, c0):
        r = lax.broadcasted_iota(jnp.int32, (tq, width), 0) + past
        c = lax.broadcasted_iota(jnp.int32, (tq, width), 1) + c0
        return c // CHUNK <= r // CHUNK, jnp.abs(r - c).astype(F32)

    vis_c, dist_c = geometry(past, 0)
    vis_n, dist_n = geometry(tq, past)
    lam = _lam(lam_ref, lam_init)
    for h in range(DIFF_HEADS):
        accs = []
        for a in range(2):
            idx = a * DIFF_HEADS + h
            qt = qz_ref[:, idx * LANES:(idx + 1) * LANES]
            kt_i = a * (DIFF_HEADS // 2) + h // 2
            ks = slice(kt_i * LANES, (kt_i + 1) * LANES)
            s_c = jnp.where(vis_c, _dot_nt(qt, ck_ref[:, ks].astype(BF16)) - _SLOPES2[h] * dist_c, -jnp.inf)
            s_n = jnp.where(vis_n, _dot_nt(qt, nk_ref[:, ks].astype(BF16)) - _SLOPES2[h] * dist_n, -jnp.inf)
            m = jnp.maximum(jnp.max(s_c, axis=1, keepdims=True), jnp.max(s_n, axis=1, keepdims=True))
            vs = slice(h * LANES, (h + 1) * LANES)
            accs.append(_dot(jnp.exp2(s_c - m).astype(BF16), _vext(cv_ref[:, vs].astype(BF16)))
                        + _dot(jnp.exp2(s_n - m).astype(BF16), _vext(nv_ref[:, vs].astype(BF16))))
        o_ref[:, h * LANES:(h + 1) * LANES] = _finish_head(accs[0], accs[1], lam, gsub_ref[...], lam_init)


def _attn_sample(qz, cache_k, cache_v, kb, vb, lambdas, gsub, lam_init):
    streams, past, _ = cache_k.shape
    n = qz.shape[0]
    tq = n // streams
    kernel = functools.partial(_attn_sample_kernel, lam_init=lam_init, past=past)
    new = pl.BlockSpec((tq, MAIN_WIDTH), lambda b: (b, 0))
    cache = pl.BlockSpec((None, past, MAIN_WIDTH), lambda b: (b, 0, 0))
    return pl.pallas_call(
        kernel,
        out_shape=jax.ShapeDtypeStruct((n, MAIN_WIDTH), BF16),
        grid=(streams,),
        in_specs=[
            pl.BlockSpec((tq, QZ_WIDTH), lambda b: (b, 0)),
            cache, cache, new, new,
            pl.BlockSpec((4, DIFF_HEAD_DIM), lambda b: (0, 0)),
            pl.BlockSpec((1, DIFF_V_DIM), lambda b: (0, 0)),
        ],
        out_specs=new,
        compiler_params=_params("arbitrary"),
        name="attn_sample",
    )(qz, cache_k, cache_v, kb, vb, lambdas, gsub)


def _memattn_kernel(mq_ref, mk_ref, mv_ref, o_ref):
    lane = lax.broadcasted_iota(jnp.int32, (mq_ref.shape[0], LANES), 1)
    for pair in range(MEM_HEADS // 2):
        ks = slice(pair * LANES, (pair + 1) * LANES)
        mk = mk_ref[:, ks].astype(BF16)
        mv = mv_ref[:, ks].astype(BF16)
        outs = []
        for h in (2 * pair, 2 * pair + 1):
            s = _dot_nt(mq_ref[:, h * LANES:(h + 1) * LANES], mk)
            p = jnp.exp2(s - jnp.max(s, axis=1, keepdims=True))
            l = jnp.sum(p, axis=1, keepdims=True)
            outs.append(_dot(p.astype(BF16), mv) / l)
        o_ref[:, ks] = jnp.where(lane < MEM_HEAD_DIM, outs[0], outs[1]).astype(BF16)


def _memattn(mq, mk, mv, l):
    n = mq.shape[0]
    groups = mk.shape[1]
    tg = n // groups
    tq = min(MEM_TILE, tg)
    per = tg // tq
    mem = pl.BlockSpec((None, None, N_MEM, MEM_WIDTH), lambda i: (l, i // per, 0, 0))
    return pl.pallas_call(
        _memattn_kernel,
        out_shape=jax.ShapeDtypeStruct((n, MEM_WIDTH), BF16),
        grid=(n // tq,),
        in_specs=[pl.BlockSpec((tq, MQZ_WIDTH), lambda i: (i, 0)), mem, mem],
        out_specs=pl.BlockSpec((tq, MEM_WIDTH), lambda i: (i, 0)),
        compiler_params=_params("arbitrary"),
        name="memattn",
    )(mq, mk, mv)


def _outproj_kernel(main_ref, mem_ref, x_ref, wm_ref, we_ref, g_ref, o_ref):
    y = _dot(main_ref[...], wm_ref[...]) + _dot(mem_ref[...], we_ref[...])
    o_ref[...] = x_ref[...] + _rms(y, g_ref[...])


def _outproj(main, mem, x, w_main, lm, w_mem, l, g):
    n = x.shape[0]
    tm = min(TOKEN_TILE, n)
    wm = main.shape[1]
    return pl.pallas_call(
        _outproj_kernel,
        out_shape=jax.ShapeDtypeStruct((n, D_MODEL), F32),
        grid=(n // tm,),
        in_specs=[
            pl.BlockSpec((tm, wm), lambda i: (i, 0)),
            pl.BlockSpec((tm, MEM_WIDTH), lambda i: (i, 0)),
            pl.BlockSpec((tm, D_MODEL), lambda i: (i, 0)),
            _resident((None, wm, D_MODEL), lambda i: (lm, 0, 0)),
            _resident((None, MEM_WIDTH, D_MODEL), lambda i: (l, 0, 0)),
            pl.BlockSpec((1, D_MODEL), lambda i: (0, 0)),
        ],
        out_specs=pl.BlockSpec((tm, D_MODEL), lambda i: (i, 0)),
        compiler_params=_params("arbitrary"),
        name="outproj",
    )(main, mem, x, w_main, w_mem, g)


def _pad_heads(w, heads, dim, pad):
    lead = w.shape[:-1]
    w = w.reshape(lead + (heads, dim))
    w = jnp.pad(w, [(0, 0)] * len(lead) + [(0, 0), (0, pad - dim)])
    return w.reshape(lead + (heads * pad,))


def _interleave64(w, heads):
    lead = w.shape[:-1]
    w = w.reshape(lead + (heads, 64))
    z = jnp.zeros_like(w)
    even = jnp.concatenate([w, z], axis=-1)
    odd = jnp.concatenate([z, w], axis=-1)
    sel = (jnp.arange(heads) % 2 == 0)[:, None]
    return jnp.where(sel, even, odd).reshape(lead + (heads * LANES,))


def _prep_weights(w_up, w_down, norm_g, w_in_a, b_gates, g_head_a, w_in_b, w_kv, w_mem_kv, w_out):
    mw = MAIN_WIDTH
    prep = {
        "w_up": w_up.astype(BF16),
        "w_down": w_down.astype(BF16),
        "norm_g": norm_g.reshape(DEPTH, 6, 1, D_MODEL),
        "w_kv": w_kv.astype(BF16),
        "w_mem_kv": w_mem_kv.astype(BF16),
        "w_out_mem": w_out[:, mw:, :].astype(BF16),
        "w_out_b": w_out[N_A_LAYERS:, :mw, :].astype(BF16),
    }
    wa = w_out[:N_A_LAYERS, :mw, :].reshape(N_A_LAYERS, MLSTM_HEADS, MLSTM_HEAD_DIM, D_MODEL)
    wa = jnp.pad(wa, ((0, 0), (0, 0), (0, HEAD_PAD - MLSTM_HEAD_DIM), (0, 0)))
    prep["w_out_a"] = wa.reshape(N_A_LAYERS, A_WIDTH, D_MODEL).astype(BF16)
    pad_a = lambda w: _pad_heads(w, MLSTM_HEADS, MLSTM_HEAD_DIM, HEAD_PAD)
    q, k, v, o = (pad_a(w_in_a[:, :, s * mw:(s + 1) * mw]) for s in range(4))
    gates = w_in_a[:, :, 4 * mw:4 * mw + 2 * MLSTM_HEADS]
    mq = _interleave64(w_in_a[:, :, 4 * mw + 2 * MLSTM_HEADS:], MEM_HEADS)
    gates_pad = jnp.pad(gates, ((0, 0), (0, 0), (0, LANES - 2 * MLSTM_HEADS)))
    prep["wrow_a"] = jnp.concatenate([q, v, o, mq, gates_pad], axis=-1).astype(BF16)
    prep["wkt_a"] = jnp.swapaxes(k, 1, 2).astype(BF16)
    prep["wgt_a"] = jnp.swapaxes(gates, 1, 2).astype(BF16)
    prep["brow_a"] = jnp.pad(b_gates, ((0, 0), (0, LANES - 2 * MLSTM_HEADS))).reshape(N_A_LAYERS, 1, LANES)
    prep["bcol_a"] = b_gates.reshape(N_A_LAYERS, 2 * MLSTM_HEADS, 1)
    prep["ghead_a"] = jnp.pad(g_head_a, ((0, 0), (0, 0), (0, HEAD_PAD - MLSTM_HEAD_DIM))).reshape(N_A_LAYERS, 1, A_WIDTH)
    nq = 2 * DIFF_HEADS * DIFF_HEAD_DIM
    prep["w_in_b"] = jnp.concatenate(
        [_interleave64(w_in_b[:, :, :nq], 2 * DIFF_HEADS), _interleave64(w_in_b[:, :, nq:], MEM_HEADS)],
        axis=-1).astype(BF16)
    return prep


def _trunk(x, groups, mem_k, mem_v, ct0, m0, cache_k, cache_v, W, lambdas, g_sub_b, g_kv):
    prompt = cache_k is None
    g = W["norm_g"]
    states = []
    k_new = v_new = kb = vb = None
    for l in range(DEPTH):
        x = _ffn(x, g[l, 0], W["w_up"], W["w_down"], g[l, 1], l, 0)
        if l < N_A_LAYERS:
            q, kt, v, o, mq, gcol, grow = _inproj_a(
                x, g[l, 2], W["wrow_a"], W["wkt_a"], W["wgt_a"], W["brow_a"], W["bcol_a"], l, groups)
            main, ctn, mn = _mlstm(q, kt, v, o, gcol, grow, W["ghead_a"], ct0, m0, l, groups, prompt)
            states.append((ctn, mn))
            w_main, lm = W["w_out_a"], l
        else:
            lb = l - N_A_LAYERS
            lam_init = 0.8 - 0.6 * math.exp(-0.3 * l)
            qz, mq = _inproj_b(x, g[l, 2], W["w_in_b"], lb)
            if prompt:
                main = _attn_prompt(qz, kb, vb, lambdas[lb], g_sub_b[lb], lam_init, groups)
            else:
                main = _attn_sample(qz, cache_k, cache_v, kb, vb, lambdas[lb], g_sub_b[lb], lam_init)
            w_main, lm = W["w_out_b"], lb
        mem = _memattn(mq, mem_k, mem_v, l)
        x = _outproj(main, mem, x, w_main, lm, W["w_out_mem"], l, g[l, 3])
        x = _ffn(x, g[l, 4], W["w_up"], W["w_down"], g[l, 5], l, 1)
        if l == N_A_LAYERS - 1:
            k_new, v_new, kb, vb = _kvproj(x, g_kv, W["w_kv"])
    return x, states, k_new, v_new


def _unpack_states(states, lead):
    hd = MLSTM_HEAD_DIM
    ct = jnp.stack([s[0] for s in states])
    m = jnp.stack([s[1] for s in states])
    c = jnp.swapaxes(ct[..., :hd, :hd], -1, -2)
    n = ct[..., :hd, N_COL]
    return c, n, m[..., :MLSTM_HEADS, 0]


def kernel(x_prompt, x_sample, mem_prompt, cache_mem_k, cache_mem_v, state_C, state_n, state_m, cache_k, cache_v,
           w_up, w_down, norm_g, w_in_a, b_gates, g_head_a, w_in_b, lambdas, g_sub_b, g_kv, w_kv, w_mem_kv, w_out):
    W = _prep_weights(w_up, w_down, norm_g, w_in_a, b_gates, g_head_a, w_in_b, w_kv, w_mem_kv, w_out)
    g_kv2 = g_kv.reshape(1, D_MODEL)
    g_sub = g_sub_b.reshape(-1, 1, DIFF_V_DIM)
    hd = MLSTM_HEAD_DIM

    bp, tp, _ = x_prompt.shape
    mk_p, mv_p = _memkv(mem_prompt.reshape(bp * N_MEM, D_MODEL), W["w_mem_kv"])
    mk_p4 = mk_p.reshape(DEPTH, bp, N_MEM, MEM_WIDTH)
    mv_p4 = mv_p.reshape(DEPTH, bp, N_MEM, MEM_WIDTH)
    ct0 = jnp.zeros((N_A_LAYERS, bp, MLSTM_HEADS, HEAD_PAD, HEAD_PAD), F32)
    m0 = jnp.zeros((N_A_LAYERS, bp, 8, LANES), F32)
    y_p, st_p, k_p, v_p = _trunk(x_prompt.reshape(bp * tp, D_MODEL), bp, mk_p4, mv_p4, ct0, m0, None, None,
                                 W, lambdas, g_sub, g_kv2)
    c_p, n_p, m_p = _unpack_states(st_p, bp)

    bs, ts, _ = x_sample.shape
    pad = HEAD_PAD - hd
    ct_s = jnp.pad(jnp.swapaxes(state_C, -1, -2), [(0, 0)] * 3 + [(0, pad), (0, pad)])
    ct_s = ct_s.at[..., :hd, N_COL].set(state_n)
    m_s = jnp.broadcast_to(jnp.pad(state_m, ((0, 0), (0, 0), (0, 8 - MLSTM_HEADS)))[..., None],
                           (N_A_LAYERS, bs, 8, LANES))
    y_s, st_s, k_s, v_s = _trunk(
        x_sample.reshape(bs * ts, D_MODEL), 1,
        cache_mem_k.reshape(DEPTH, bs, N_MEM, MEM_WIDTH), cache_mem_v.reshape(DEPTH, bs, N_MEM, MEM_WIDTH),
        ct_s, m_s, cache_k.reshape(bs, -1, MAIN_WIDTH), cache_v.reshape(bs, -1, MAIN_WIDTH),
        W, lambdas, g_sub, g_kv2)
    c_s, n_s, m_sn = _unpack_states(st_s, bs)

    kshape = (2 * DIFF_HEADS, DIFF_HEAD_DIM)
    vshape = (DIFF_HEADS, DIFF_V_DIM)
    mshape = (DEPTH, bp, N_MEM, MEM_HEADS, MEM_HEAD_DIM)
    return (y_p.reshape(bp, tp, D_MODEL), y_s.reshape(bs, ts, D_MODEL),
            mk_p.reshape(mshape), mv_p.reshape(mshape),
            c_p, n_p, m_p,
            k_p.reshape((bp, tp) + kshape), v_p.reshape((bp, tp) + vshape),
            c_s, n_s, m_sn,
            k_s.reshape((bs, ts) + kshape), v_s.reshape((bs, ts) + vshape))
```

```python
import functools
import math

import jax
import jax.numpy as jnp
from jax import lax
from jax.experimental import pallas as pl
from jax.experimental.pallas import tpu as pltpu

F32 = jnp.float32
BF16 = jnp.bfloat16

D_MODEL = 1024
DEPTH = 4
CHUNK = 64
N_MEM = 256
N_A_LAYERS = DEPTH // 2
MEM_HEADS = 4
MEM_WIDTH = 256
MEM_HEAD_DIM = 64
MAIN_WIDTH = 768
MLSTM_HEADS = 4
MLSTM_HEAD_DIM = 192
DIFF_HEAD_DIM = 64
DIFF_V_DIM = 128
DIFF_HEADS = 6
FFN_DIM = 2816
EPS = 1e-6

LANES = 128
HEAD_PAD = 256
N_COL = MLSTM_HEAD_DIM
A_WIDTH = MLSTM_HEADS * HEAD_PAD
QZ_WIDTH = 2 * DIFF_HEADS * LANES
MQZ_WIDTH = MEM_HEADS * LANES
LOG2E = math.log2(math.e)
VMEM_LIMIT = 56 * 1024 * 1024

TOKEN_TILE = 512
MLSTM_TILE = 512
MLSTM_SEQ_CHUNK = 256
ATTN_TQ = 512
MEM_TILE = 512


def _params(*sem):
    return pltpu.CompilerParams(dimension_semantics=sem, vmem_limit_bytes=VMEM_LIMIT)


def _rms(x, g):
    return x * lax.rsqrt(jnp.mean(x * x, axis=-1, keepdims=True) + EPS) * g


def _log_sigmoid(x):
    return jnp.minimum(x, 0.0) - jnp.log1p(jnp.exp(-jnp.abs(x)))


def _dot(a, b):
    return jnp.dot(a, b, preferred_element_type=F32)


def _dot_nt(a, b):
    return lax.dot_general(a, b, (((1,), (1,)), ((), ())), preferred_element_type=F32)


def _resident(shape, index_map):
    return pl.BlockSpec(shape, index_map, pipeline_mode=pl.Buffered(1))


def _ffn_kernel(x_ref, g0_ref, wg_ref, wu_ref, wd_ref, g1_ref, o_ref):
    x = x_ref[...]
    xb = _rms(x, g0_ref[...]).astype(BF16)
    hg = _dot(xb, wg_ref[...])
    hu = _dot(xb, wu_ref[...])
    a = (hg * jax.nn.sigmoid(hg) * hu).astype(BF16)
    y = _dot(a, wd_ref[...])
    o_ref[...] = x + 0.5 * _rms(y, g1_ref[...])


def _ffn(x, g0, w_up, w_down, g1, l, k):
    n = x.shape[0]
    tm = min(TOKEN_TILE, n)
    return pl.pallas_call(
        _ffn_kernel,
        out_shape=jax.ShapeDtypeStruct((n, D_MODEL), F32),
        grid=(n // tm,),
        in_specs=[
            pl.BlockSpec((tm, D_MODEL), lambda i: (i, 0)),
            pl.BlockSpec((1, D_MODEL), lambda i: (0, 0)),
            _resident((None, None, D_MODEL, FFN_DIM), lambda i: (l, k, 0, 0)),
            _resident((None, None, D_MODEL, FFN_DIM), lambda i: (l, k, 0, 1)),
            _resident((None, None, FFN_DIM, D_MODEL), lambda i: (l, k, 0, 0)),
            pl.BlockSpec((1, D_MODEL), lambda i: (0, 0)),
        ],
        out_specs=pl.BlockSpec((tm, D_MODEL), lambda i: (i, 0)),
        compiler_params=_params("arbitrary"),
        name="ffn",
    )(x, g0, w_up, w_up, w_down, g1)


def _inproj_a_kernel(x_ref, g_ref, wrow_ref, wkt_ref, wgt_ref, brow_ref, bcol_ref,
                     q_ref, kt_ref, v_ref, o_ref, mq_ref, gcol_ref, grow_ref):
    xb = _rms(x_ref[...], g_ref[...]).astype(BF16)
    y = _dot(xb, wrow_ref[...])
    q_ref[...] = y[:, 0:A_WIDTH].astype(BF16)
    v = y[:, A_WIDTH:2 * A_WIDTH]
    lane = lax.broadcasted_iota(jnp.int32, v.shape, 1)
    v_ref[...] = jnp.where(lane % HEAD_PAD == N_COL, 1.0, v).astype(BF16)
    o_ref[...] = y[:, 2 * A_WIDTH:3 * A_WIDTH]
    mq_ref[...] = (y[:, 3 * A_WIDTH:3 * A_WIDTH + MQZ_WIDTH] * (MEM_HEAD_DIM ** -0.5 * LOG2E)).astype(BF16)
    g = y[:, 3 * A_WIDTH + MQZ_WIDTH:] + brow_ref[...]
    glane = lax.broadcasted_iota(jnp.int32, g.shape, 1)
    gcol_ref[...] = jnp.where(glane < MLSTM_HEADS, g, _log_sigmoid(g))
    kt = _dot_nt(wkt_ref[...], xb)
    kt_ref[...] = (kt * (MLSTM_HEAD_DIM ** -0.5)).astype(BF16)
    gt = _dot_nt(wgt_ref[...], xb) + bcol_ref[...]
    grow_idx = lax.broadcasted_iota(jnp.int32, gt.shape, 0)
    grow_ref[...] = jnp.where(grow_idx < MLSTM_HEADS, gt, _log_sigmoid(gt))


def _inproj_a(x, g, wrow, wkt, wgt, brow, bcol, l, groups):
    n = x.shape[0]
    tg = n // groups
    tm = min(TOKEN_TILE, tg)
    per = tg // tm
    wcols = wrow.shape[2]
    outs = (
        jax.ShapeDtypeStruct((n, A_WIDTH), BF16),
        jax.ShapeDtypeStruct((groups, A_WIDTH, tg), BF16),
        jax.ShapeDtypeStruct((n, A_WIDTH), BF16),
        jax.ShapeDtypeStruct((n, A_WIDTH), F32),
        jax.ShapeDtypeStruct((n, MQZ_WIDTH), BF16),
        jax.ShapeDtypeStruct((n, LANES), F32),
        jax.ShapeDtypeStruct((groups, 8, tg), F32),
    )
    row = lambda w: pl.BlockSpec((tm, w), lambda i: (i, 0))
    return pl.pallas_call(
        _inproj_a_kernel,
        out_shape=outs,
        grid=(n // tm,),
        in_specs=[
            row(D_MODEL),
            pl.BlockSpec((1, D_MODEL), lambda i: (0, 0)),
            _resident((None, D_MODEL, wcols), lambda i: (l, 0, 0)),
            _resident((None, A_WIDTH, D_MODEL), lambda i: (l, 0, 0)),
            _resident((None, 8, D_MODEL), lambda i: (l, 0, 0)),
            pl.BlockSpec((None, 1, LANES), lambda i: (l, 0, 0)),
            pl.BlockSpec((None, 8, 1), lambda i: (l, 0, 0)),
        ],
        out_specs=(
            row(A_WIDTH),
            pl.BlockSpec((None, A_WIDTH, tm), lambda i: (i // per, 0, i % per)),
            row(A_WIDTH),
            row(A_WIDTH),
            row(MQZ_WIDTH),
            row(LANES),
            pl.BlockSpec((None, 8, tm), lambda i: (i // per, 0, i % per)),
        ),
        compiler_params=_params("arbitrary"),
        name="inproj_a",
    )(x, g, wrow, wkt, wgt, brow, bcol)


def _inproj_b_kernel(x_ref, g_ref, w_ref, qz_ref, mq_ref):
    xb = _rms(x_ref[...], g_ref[...]).astype(BF16)
    y = _dot(xb, w_ref[...])
    qz_ref[...] = (y[:, :QZ_WIDTH] * (DIFF_HEAD_DIM ** -0.5 * LOG2E)).astype(BF16)
    mq_ref[...] = (y[:, QZ_WIDTH:] * (MEM_HEAD_DIM ** -0.5 * LOG2E)).astype(BF16)


def _inproj_b(x, g, w, lb):
    n = x.shape[0]
    tm = min(TOKEN_TILE, n)
    return pl.pallas_call(
        _inproj_b_kernel,
        out_shape=(jax.ShapeDtypeStruct((n, QZ_WIDTH), BF16), jax.ShapeDtypeStruct((n, MQZ_WIDTH), BF16)),
        grid=(n // tm,),
        in_specs=[
            pl.BlockSpec((tm, D_MODEL), lambda i: (i, 0)),
            pl.BlockSpec((1, D_MODEL), lambda i: (0, 0)),
            _resident((None, D_MODEL, QZ_WIDTH + MQZ_WIDTH), lambda i: (lb, 0, 0)),
        ],
        out_specs=(pl.BlockSpec((tm, QZ_WIDTH), lambda i: (i, 0)), pl.BlockSpec((tm, MQZ_WIDTH), lambda i: (i, 0))),
        compiler_params=_params("arbitrary"),
        name="inproj_b",
    )(x, g, w)


def _kvproj_kernel(x_ref, g_ref, w_ref, k_ref, v_ref, kb_ref, vb_ref):
    xb = _rms(x_ref[...], g_ref[...]).astype(BF16)
    y = _dot(xb, w_ref[...])
    k = y[:, :MAIN_WIDTH]
    v = y[:, MAIN_WIDTH:]
    k_ref[...] = k
    v_ref[...] = v
    kb_ref[...] = k.astype(BF16)
    vb_ref[...] = v.astype(BF16)


def _kvproj(x, g, w):
    n = x.shape[0]
    tm = min(TOKEN_TILE, n)
    blk = pl.BlockSpec((tm, MAIN_WIDTH), lambda i: (i, 0))
    return pl.pallas_call(
        _kvproj_kernel,
        out_shape=(jax.ShapeDtypeStruct((n, MAIN_WIDTH), F32), jax.ShapeDtypeStruct((n, MAIN_WIDTH), F32),
                   jax.ShapeDtypeStruct((n, MAIN_WIDTH), BF16), jax.ShapeDtypeStruct((n, MAIN_WIDTH), BF16)),
        grid=(n // tm,),
        in_specs=[
            pl.BlockSpec((tm, D_MODEL), lambda i: (i, 0)),
            pl.BlockSpec((1, D_MODEL), lambda i: (0, 0)),
            _resident((D_MODEL, 2 * MAIN_WIDTH), lambda i: (0, 0)),
        ],
        out_specs=(blk, blk, blk, blk),
        compiler_params=_params("arbitrary"),
        name="kvproj",
    )(x, g, w)


def _memkv_kernel(m_ref, w_ref, k_ref, v_ref):
    y = _dot(m_ref[...].astype(BF16), w_ref[...])
    k_ref[...] = y[:, :MEM_WIDTH]
    v_ref[...] = y[:, MEM_WIDTH:]


def _memkv(mem, w):
    n = mem.shape[0]
    out = jax.ShapeDtypeStruct((DEPTH, n, MEM_WIDTH), F32)
    blk = pl.BlockSpec((None, n, MEM_WIDTH), lambda l: (l, 0, 0))
    return pl.pallas_call(
        _memkv_kernel,
        out_shape=(out, out),
        grid=(DEPTH,),
        in_specs=[pl.BlockSpec((n, D_MODEL), lambda l: (0, 0)),
                  pl.BlockSpec((None, D_MODEL, 2 * MEM_WIDTH), lambda l: (l, 0, 0))],
        out_specs=(blk, blk),
        compiler_params=_params("arbitrary"),
        name="memkv",
    )(mem, w)


def _split3(x):
    hi = x.astype(BF16)
    r = x - hi.astype(F32)
    mid = r.astype(BF16)
    lo = (r - mid.astype(F32)).astype(BF16)
    return hi, mid, lo


def _mlstm_kernel(q_ref, kt_ref, v_ref, o_ref, gcol_ref, grow_ref, gh_ref, ct0_ref, m0_ref,
                  h_ref, ctn_ref, mn_ref, ct_s, m_s, tri_s, *, sequential, n_chunks, chunk):
    t = pl.program_id(1)
    tb = n_chunks * chunk
    win = max(chunk, LANES)

    @pl.when((pl.program_id(0) == 0) & (t == 0))
    def _init_tri():
        r = lax.broadcasted_iota(jnp.int32, (tb, tb), 0)
        c = lax.broadcasted_iota(jnp.int32, (tb, tb), 1)
        tri_s[...] = jnp.where((r // chunk == c // chunk) & (c <= r), 1.0, 0.0).astype(BF16)

    @pl.when((t == 0) | (not sequential))
    def _init_state():
        ct_s[...] = ct0_ref[...]
        m_s[...] = m0_ref[...]

    tri = tri_s[...]
    gcol = gcol_ref[...]
    grow = grow_ref[...]
    bcol_all = sum(_dot(tri, part) for part in _split3(gcol))
    brow_all = sum(_dot_nt(part, tri) for part in _split3(grow))

    row_i = lax.broadcasted_iota(jnp.int32, (chunk, win), 0)
    lane_i = lax.broadcasted_iota(jnp.int32, (chunk, win), 1)
    lane_row = lax.broadcasted_iota(jnp.int32, (1, win), 1)
    pad_lane = lax.broadcasted_iota(jnp.int32, (chunk, HEAD_PAD), 1)

    for c in range(n_chunks):
        slot = 0 if sequential else c
        r0 = c * chunk
        p0 = r0 // win * win
        off = r0 - p0
        in_chunk = (lane_i >= off) & (lane_i < off + chunk)
        vis = in_chunk & (lane_i - off <= row_i)
        in_chunk_row = (lane_row >= off) & (lane_row < off + chunk)
        for h in range(MLSTM_HEADS):
            hs = slice(h * HEAD_PAD, (h + 1) * HEAD_PAD)
            fl = MLSTM_HEADS + h
            bcol = bcol_all[r0:r0 + chunk, fl:fl + 1]
            brow = brow_all[fl:fl + 1, p0:p0 + win]
            igrow = grow[h:h + 1, p0:p0 + win]
            m_prev = m_s[slot, h:h + 1, 0:1]
            dlog = jnp.where(vis, bcol - brow + igrow, -jnp.inf)
            a = jnp.max(dlog, axis=1, keepdims=True)
            mt = jnp.maximum(bcol + m_prev, a)
            e = jnp.exp(dlog - mt)
            w_inter = jnp.exp(bcol + m_prev - mt)
            qh = q_ref[r0:r0 + chunk, hs]
            kth = kt_ref[hs, p0:p0 + win]
            vh = v_ref[p0:p0 + win, hs]
            s = (_dot(qh, kth) * e).astype(BF16)
            ct = ct_s[slot, h]
            num = w_inter * _dot(qh, ct.astype(BF16)) + _dot(s, vh)
            nq = num[:, N_COL:N_COL + 1]
            den = jnp.maximum(jnp.abs(nq), jnp.exp(-mt))
            hh = jnp.where(pad_lane < MLSTM_HEAD_DIM, num / den, 0.0)
            ms = jnp.sum(hh * hh, axis=1, keepdims=True) * (1.0 / MLSTM_HEAD_DIM)
            hn = hh * lax.rsqrt(ms + EPS) * gh_ref[:, hs]
            h_ref[r0:r0 + chunk, hs] = (hn * jax.nn.sigmoid(o_ref[r0:r0 + chunk, hs])).astype(BF16)
            last = off + chunk - 1
            b_last = brow[:, last:last + 1]
            m_new = mt[chunk - 1:chunk, :]
            w_end = jnp.where(in_chunk_row, jnp.exp(b_last - brow + igrow - m_new), 0.0)
            decay = jnp.exp(b_last + m_prev - m_new)
            kw = (kth.astype(F32) * w_end).astype(BF16)
            ct_s[slot, h] = decay * ct + _dot(kw, vh)
            m_s[slot, h:h + 1, :] = jnp.broadcast_to(m_new, (1, LANES))

    @pl.when((t == pl.num_programs(1) - 1) | (not sequential))
    def _fin():
        ctn_ref[...] = ct_s[...]
        mn_ref[...] = m_s[...]


def _mlstm(q, kt, v, o, gcol, grow, ghead, ct0, m0, l, groups, sequential):
    n = q.shape[0]
    tg = n // groups
    tb = min(MLSTM_TILE, tg)
    chunk = min(MLSTM_SEQ_CHUNK, tb) if sequential else CHUNK
    n_chunks = tb // chunk
    per = tg // tb
    slots = 1 if sequential else n_chunks
    streams = ct0.shape[1]
    sblk = (lambda g, t: g) if sequential else (lambda g, t: g * per + t)
    row = lambda w: pl.BlockSpec((tb, w), lambda g, t: (g * per + t, 0))
    kernel = functools.partial(_mlstm_kernel, sequential=sequential, n_chunks=n_chunks, chunk=chunk)
    return pl.pallas_call(
        kernel,
        out_shape=(jax.ShapeDtypeStruct((n, A_WIDTH), BF16),
                   jax.ShapeDtypeStruct((streams, MLSTM_HEADS, HEAD_PAD, HEAD_PAD), F32),
                   jax.ShapeDtypeStruct((streams, 8, LANES), F32)),
        grid=(groups, per),
        in_specs=[
            row(A_WIDTH),
            pl.BlockSpec((None, A_WIDTH, tb), lambda g, t: (g, 0, t)),
            row(A_WIDTH),
            row(A_WIDTH),
            row(LANES),
            pl.BlockSpec((None, 8, tb), lambda g, t: (g, 0, t)),
            pl.BlockSpec((None, 1, A_WIDTH), lambda g, t: (l, 0, 0)),
            pl.BlockSpec((None, slots, MLSTM_HEADS, HEAD_PAD, HEAD_PAD), lambda g, t: (l, sblk(g, t), 0, 0, 0)),
            pl.BlockSpec((None, slots, 8, LANES), lambda g, t: (l, sblk(g, t), 0, 0)),
        ],
        out_specs=(
            row(A_WIDTH),
            pl.BlockSpec((slots, MLSTM_HEADS, HEAD_PAD, HEAD_PAD), lambda g, t: (sblk(g, t), 0, 0, 0)),
            pl.BlockSpec((slots, 8, LANES), lambda g, t: (sblk(g, t), 0, 0)),
        ),
        scratch_shapes=[
            pltpu.VMEM((slots, MLSTM_HEADS, HEAD_PAD, HEAD_PAD), F32),
            pltpu.VMEM((slots, 8, LANES), F32),
            pltpu.VMEM((tb, tb), BF16),
        ],
        compiler_params=_params("arbitrary", "arbitrary"),
        name="mlstm",
    )(q, kt, v, o, gcol, grow, ghead, ct0, m0)


_SLOPES2 = tuple(LOG2E * 2.0 ** (-8.0 * (h + 1) / DIFF_HEADS) for h in range(DIFF_HEADS))


def _lam(lam_ref, lam_init):
    lp = lam_ref[...]
    s1 = jnp.sum(lp[0:1] * lp[1:2], axis=1, keepdims=True)
    s2 = jnp.sum(lp[2:3] * lp[3:4], axis=1, keepdims=True)
    return jnp.exp(s1) - jnp.exp(s2) + lam_init


def _vext(v_tile):
    return jnp.concatenate([v_tile, jnp.ones_like(v_tile)], axis=1)


def _finish_head(acc1, acc2, lam, gsub, lam_init):
    o = acc1[:, :DIFF_V_DIM] / acc1[:, DIFF_V_DIM:] - lam * (acc2[:, :DIFF_V_DIM] / acc2[:, DIFF_V_DIM:])
    return (_rms(o, gsub) * (1.0 - lam_init)).astype(BF16)


def _attn_prompt_kernel(it_ref, jt_ref, qz_ref, k_ref, v_ref, lam_ref, gsub_ref, o_ref, acc_s, m_s, *, lam_init, tq):
    step_id = pl.program_id(1)
    i = it_ref[step_id]
    j = jt_ref[step_id]
    tk = 2 * tq
    last = j == (i >> 1)
    odd = (i & 1) == 1

    @pl.when(j == 0)
    def _init():
        acc_s[...] = jnp.zeros_like(acc_s)
        m_s[...] = jnp.full_like(m_s, -jnp.inf)

    def step(keys, bias_fn):
        for a in range(2):
            for h in range(DIFF_HEADS):
                idx = a * DIFF_HEADS + h
                qt = qz_ref[:, idx * LANES:(idx + 1) * LANES]
                kt_i = a * (DIFF_HEADS // 2) + h // 2
                kt = k_ref[0:keys, kt_i * LANES:(kt_i + 1) * LANES]
                s = bias_fn(_dot_nt(qt, kt), h)
                m_prev = m_s[idx]
                m_new = jnp.maximum(m_prev, jnp.max(s, axis=1, keepdims=True))
                p = jnp.exp2(s - jnp.concatenate([m_new] * (keys // LANES), axis=1)).astype(BF16)
                alpha = jnp.exp2(m_prev - m_new)
                vt = _vext(v_ref[0:keys, h * LANES:(h + 1) * LANES])
                acc_s[idx] = jnp.concatenate([alpha, alpha], axis=1) * acc_s[idx] + _dot(p, vt)
                m_s[idx] = m_new

    def finish():
        lam = _lam(lam_ref, lam_init)
        for h in range(DIFF_HEADS):
            o_ref[:, h * LANES:(h + 1) * LANES] = _finish_head(
                acc_s[h], acc_s[DIFF_HEADS + h], lam, gsub_ref[...], lam_init)

    def own_keys(keys):
        r = lax.broadcasted_iota(jnp.int32, (tq, keys), 0)
        c = lax.broadcasted_iota(jnp.int32, (tq, keys), 1) - (keys - tq)
        vis = (c < 0) | (jnp.maximum(c, 0) // CHUNK <= r // CHUNK)
        g = (r - jnp.abs(r - c)).astype(F32)
        step(keys, lambda s, h: jnp.where(vis, s + _SLOPES2[h] * g, -jnp.inf))
        finish()

    @pl.when(jnp.logical_not(last))
    def _past():
        col = lax.broadcasted_iota(jnp.int32, (1, tk), 1).astype(F32)
        off = (j * tk - i * tq).astype(F32)
        step(tk, lambda s, h: s + _SLOPES2[h] * (col + off))

    @pl.when(last & odd)
    def _diag_odd():
        own_keys(tk)

    @pl.when(last & jnp.logical_not(odd))
    def _diag_even():
        own_keys(tq)


def _attn_prompt(qz, kb, vb, lambdas, gsub, lam_init, groups):
    n = qz.shape[0]
    tg = n // groups
    tq = min(ATTN_TQ, tg // 2)
    tk = 2 * tq
    nq = tg // tq
    nkv = tg // tk
    steps = [(i, j) for i in range(nq) for j in range(i // 2 + 1)]
    itab = jnp.asarray([s[0] for s in steps], jnp.int32)
    jtab = jnp.asarray([s[1] for s in steps], jnp.int32)
    kernel = functools.partial(_attn_prompt_kernel, lam_init=lam_init, tq=tq)
    q_spec = lambda w: pl.BlockSpec((tq, w), lambda b, s, it, jt: (b * nq + it[s], 0))
    kv_spec = pl.BlockSpec((tk, MAIN_WIDTH), lambda b, s, it, jt: (b * nkv + jt[s], 0))
    grid_spec = pltpu.PrefetchScalarGridSpec(
        num_scalar_prefetch=2,
        grid=(groups, len(steps)),
        in_specs=[
            q_spec(QZ_WIDTH),
            kv_spec,
            kv_spec,
            pl.BlockSpec((4, DIFF_HEAD_DIM), lambda b, s, it, jt: (0, 0)),
            pl.BlockSpec((1, DIFF_V_DIM), lambda b, s, it, jt: (0, 0)),
        ],
        out_specs=q_spec(MAIN_WIDTH),
        scratch_shapes=[
            pltpu.VMEM((2 * DIFF_HEADS, tq, 2 * DIFF_V_DIM), F32),
            pltpu.VMEM((2 * DIFF_HEADS, tq, LANES), F32),
        ],
    )
    return pl.pallas_call(
        kernel,
        out_shape=jax.ShapeDtypeStruct((n, MAIN_WIDTH), BF16),
        grid_spec=grid_spec,
        compiler_params=_params("arbitrary", "arbitrary"),
        name="attn_prompt",
    )(itab, jtab, qz, kb, vb, lambdas, gsub)


def _attn_sample_kernel(qz_ref, ck_ref, cv_ref, nk_ref, nv_ref, lam_ref, gsub_ref, o_ref, *, lam_init, past):
    tq = qz_ref.shape[0]

    def geometry(width, c0):
        r = lax.broadcasted_iota(jnp.int32, (tq, width), 0) + past
        c = lax.broadcasted_iota(jnp.int32, (tq, width), 1) + c0
        return c // CHUNK <= r // CHUNK, jnp.abs(r - c).astype(F32)

    vis_c, dist_c = geometry(past, 0)
    vis_n, dist_n = geometry(tq, past)
    lam = _lam(lam_ref, lam_init)
    for h in range(DIFF_HEADS):
        accs = []
        for a in range(2):
            idx = a * DIFF_HEADS + h
            qt = qz_ref[:, idx * LANES:(idx + 1) * LANES]
            kt_i = a * (DIFF_HEADS // 2) + h // 2
            ks = slice(kt_i * LANES, (kt_i + 1) * LANES)
            s_c = jnp.where(vis_c, _dot_nt(qt, ck_ref[:, ks].astype(BF16)) - _SLOPES2[h] * dist_c, -jnp.inf)
            s_n = jnp.where(vis_n, _dot_nt(qt, nk_ref[:, ks].astype(BF16)) - _SLOPES2[h] * dist_n, -jnp.inf)
            m = jnp.maximum(jnp.max(s_c, axis=1, keepdims=True), jnp.max(s_n, axis=1, keepdims=True))
            vs = slice(h * LANES, (h + 1) * LANES)
            accs.append(_dot(jnp.exp2(s_c - m).astype(BF16), _vext(cv_ref[:, vs].astype(BF16)))
                        + _dot(jnp.exp2(s_n - m).astype(BF16), _vext(nv_ref[:, vs].astype(BF16))))
        o_ref[:, h * LANES:(h + 1) * LANES] = _finish_head(accs[0], accs[1], lam, gsub_ref[...], lam_init)


def _attn_sample(qz, cache_k, cache_v, kb, vb, lambdas, gsub, lam_init):
    streams, past, _ = cache_k.shape
    n = qz.shape[0]
    tq = n // streams
    kernel = functools.partial(_attn_sample_kernel, lam_init=lam_init, past=past)
    new = pl.BlockSpec((tq, MAIN_WIDTH), lambda b: (b, 0))
    cache = pl.BlockSpec((None, past, MAIN_WIDTH), lambda b: (b, 0, 0))
    return pl.pallas_call(
        kernel,
        out_shape=jax.ShapeDtypeStruct((n, MAIN_WIDTH), BF16),
        grid=(streams,),
        in_specs=[
            pl.BlockSpec((tq, QZ_WIDTH), lambda b: (b, 0)),
            cache, cache, new, new,
            pl.BlockSpec((4, DIFF_HEAD_DIM), lambda b: (0, 0)),
            pl.BlockSpec((1, DIFF_V_DIM), lambda b: (0, 0)),
        ],
        out_specs=new,
        compiler_params=_params("arbitrary"),
        name="attn_sample",
    )(qz, cache_k, cache_v, kb, vb, lambdas, gsub)


def _memattn_kernel(mq_ref, mk_ref, mv_ref, o_ref):
    lane = lax.broadcasted_iota(jnp.int32, (mq_ref.shape[0], LANES), 1)
    for pair in range(MEM_HEADS // 2):
        ks = slice(pair * LANES, (pair + 1) * LANES)
        mk = mk_ref[:, ks].astype(BF16)
        mv = mv_ref[:, ks].astype(BF16)
        outs = []
        for h in (2 * pair, 2 * pair + 1):
            s = _dot_nt(mq_ref[:, h * LANES:(h + 1) * LANES], mk)
            p = jnp.exp2(s - jnp.max(s, axis=1, keepdims=True))
            l = jnp.sum(p, axis=1, keepdims=True)
            outs.append(_dot(p.astype(BF16), mv) / l)
        o_ref[:, ks] = jnp.where(lane < MEM_HEAD_DIM, outs[0], outs[1]).astype(BF16)


def _memattn(mq, mk, mv, l):
    n = mq.shape[0]
    groups = mk.shape[1]
    tg = n // groups
    tq = min(MEM_TILE, tg)
    per = tg // tq
    mem = pl.BlockSpec((None, None, N_MEM, MEM_WIDTH), lambda i: (l, i // per, 0, 0))
    return pl.pallas_call(
        _memattn_kernel,
        out_shape=jax.ShapeDtypeStruct((n, MEM_WIDTH), BF16),
        grid=(n // tq,),
        in_specs=[pl.BlockSpec((tq, MQZ_WIDTH), lambda i: (i, 0)), mem, mem],
        out_specs=pl.BlockSpec((tq, MEM_WIDTH), lambda i: (i, 0)),
        compiler_params=_params("arbitrary"),
        name="memattn",
    )(mq, mk, mv)


def _outproj_kernel(main_ref, mem_ref, x_ref, wm_ref, we_ref, g_ref, o_ref):
    y = _dot(main_ref[...], wm_ref[...]) + _dot(mem_ref[...], we_ref[...])
    o_ref[...] = x_ref[...] + _rms(y, g_ref[...])


def _outproj(main, mem, x, w_main, lm, w_mem, l, g):
    n = x.shape[0]
    tm = min(TOKEN_TILE, n)
    wm = main.shape[1]
    return pl.pallas_call(
        _outproj_kernel,
        out_shape=jax.ShapeDtypeStruct((n, D_MODEL), F32),
        grid=(n // tm,),
        in_specs=[
            pl.BlockSpec((tm, wm), lambda i: (i, 0)),
            pl.BlockSpec((tm, MEM_WIDTH), lambda i: (i, 0)),
            pl.BlockSpec((tm, D_MODEL), lambda i: (i, 0)),
            _resident((None, wm, D_MODEL), lambda i: (lm, 0, 0)),
            _resident((None, MEM_WIDTH, D_MODEL), lambda i: (l, 0, 0)),
            pl.BlockSpec((1, D_MODEL), lambda i: (0, 0)),
        ],
        out_specs=pl.BlockSpec((tm, D_MODEL), lambda i: (i, 0)),
        compiler_params=_params("arbitrary"),
        name="outproj",
    )(main, mem, x, w_main, w_mem, g)


def _pad_heads(w, heads, dim, pad):
    lead = w.shape[:-1]
    w = w.reshape(lead + (heads, dim))
    w = jnp.pad(w, [(0, 0)] * len(lead) + [(0, 0), (0, pad - dim)])
    return w.reshape(lead + (heads * pad,))


def _interleave64(w, heads):
    lead = w.shape[:-1]
    w = w.reshape(lead + (heads, 64))
    z = jnp.zeros_like(w)
    even = jnp.concatenate([w, z], axis=-1)
    odd = jnp.concatenate([z, w], axis=-1)
    sel = (jnp.arange(heads) % 2 == 0)[:, None]
    return jnp.where(sel, even, odd).reshape(lead + (heads * LANES,))


def _prep_weights(w_up, w_down, norm_g, w_in_a, b_gates, g_head_a, w_in_b, w_kv, w_mem_kv, w_out):
    mw = MAIN_WIDTH
    prep = {
        "w_up": w_up.astype(BF16),
        "w_down": w_down.astype(BF16),
        "norm_g": norm_g.reshape(DEPTH, 6, 1, D_MODEL),
        "w_kv": w_kv.astype(BF16),
        "w_mem_kv": w_mem_kv.astype(BF16),
        "w_out_mem": w_out[:, mw:, :].astype(BF16),
        "w_out_b": w_out[N_A_LAYERS:, :mw, :].astype(BF16),
    }
    wa = w_out[:N_A_LAYERS, :mw, :].reshape(N_A_LAYERS, MLSTM_HEADS, MLSTM_HEAD_DIM, D_MODEL)
    wa = jnp.pad(wa, ((0, 0), (0, 0), (0, HEAD_PAD - MLSTM_HEAD_DIM), (0, 0)))
    prep["w_out_a"] = wa.reshape(N_A_LAYERS, A_WIDTH, D_MODEL).astype(BF16)
    pad_a = lambda w: _pad_heads(w, MLSTM_HEADS, MLSTM_HEAD_DIM, HEAD_PAD)
    q, k, v, o = (pad_a(w_in_a[:, :, s * mw:(s + 1) * mw]) for s in range(4))
    gates = w_in_a[:, :, 4 * mw:4 * mw + 2 * MLSTM_HEADS]
    mq = _interleave64(w_in_a[:, :, 4 * mw + 2 * MLSTM_HEADS:], MEM_HEADS)
    gates_pad = jnp.pad(gates, ((0, 0), (0, 0), (0, LANES - 2 * MLSTM_HEADS)))
    prep["wrow_a"] = jnp.concatenate([q, v, o, mq, gates_pad], axis=-1).astype(BF16)
    prep["wkt_a"] = jnp.swapaxes(k, 1, 2).astype(BF16)
    prep["wgt_a"] = jnp.swapaxes(gates, 1, 2).astype(BF16)
    prep["brow_a"] = jnp.pad(b_gates, ((0, 0), (0, LANES - 2 * MLSTM_HEADS))).reshape(N_A_LAYERS, 1, LANES)
    prep["bcol_a"] = b_gates.reshape(N_A_LAYERS, 2 * MLSTM_HEADS, 1)
    prep["ghead_a"] = jnp.pad(g_head_a, ((0, 0), (0, 0), (0, HEAD_PAD - MLSTM_HEAD_DIM))).reshape(N_A_LAYERS, 1, A_WIDTH)
    nq = 2 * DIFF_HEADS * DIFF_HEAD_DIM
    prep["w_in_b"] = jnp.concatenate(
        [_interleave64(w_in_b[:, :, :nq], 2 * DIFF_HEADS), _interleave64(w_in_b[:, :, nq:], MEM_HEADS)],
        axis=-1).astype(BF16)
    return prep


def _trunk(x, groups, mem_k, mem_v, ct0, m0, cache_k, cache_v, W, lambdas, g_sub_b, g_kv):
    prompt = cache_k is None
    g = W["norm_g"]
    states = []
    k_new = v_new = kb = vb = None
    for l in range(DEPTH):
        x = _ffn(x, g[l, 0], W["w_up"], W["w_down"], g[l, 1], l, 0)
        if l < N_A_LAYERS:
            q, kt, v, o, mq, gcol, grow = _inproj_a(
                x, g[l, 2], W["wrow_a"], W["wkt_a"], W["wgt_a"], W["brow_a"], W["bcol_a"], l, groups)
            main, ctn, mn = _mlstm(q, kt, v, o, gcol, grow, W["ghead_a"], ct0, m0, l, groups, prompt)
            states.append((ctn, mn))
            w_main, lm = W["w_out_a"], l
        else:
            lb = l - N_A_LAYERS
            lam_init = 0.8 - 0.6 * math.exp(-0.3 * l)
            qz, mq = _inproj_b(x, g[l, 2], W["w_in_b"], lb)
            if prompt:
                main = _attn_prompt(qz, kb, vb, lambdas[lb], g_sub_b[lb], lam_init, groups)
            else:
                main = _attn_sample(qz, cache_k, cache_v, kb, vb, lambdas[lb], g_sub_b[lb], lam_init)
            w_main, lm = W["w_out_b"], lb
        mem = _memattn(mq, mem_k, mem_v, l)
        x = _outproj(main, mem, x, w_main, lm, W["w_out_mem"], l, g[l, 3])
        x = _ffn(x, g[l, 4], W["w_up"], W["w_down"], g[l, 5], l, 1)
        if l == N_A_LAYERS - 1:
            k_new, v_new, kb, vb = _kvproj(x, g_kv, W["w_kv"])
    return x, states, k_new, v_new


def _unpack_states(states, lead):
    hd = MLSTM_HEAD_DIM
    ct = jnp.stack([s[0] for s in states])
    m = jnp.stack([s[1] for s in states])
    c = jnp.swapaxes(ct[..., :hd, :hd], -1, -2)
    n = ct[..., :hd, N_COL]
    return c, n, m[..., :MLSTM_HEADS, 0]


def kernel(x_prompt, x_sample, mem_prompt, cache_mem_k, cache_mem_v, state_C, state_n, state_m, cache_k, cache_v,
           w_up, w_down, norm_g, w_in_a, b_gates, g_head_a, w_in_b, lambdas, g_sub_b, g_kv, w_kv, w_mem_kv, w_out):
    W = _prep_weights(w_up, w_down, norm_g, w_in_a, b_gates, g_head_a, w_in_b, w_kv, w_mem_kv, w_out)
    g_kv2 = g_kv.reshape(1, D_MODEL)
    g_sub = g_sub_b.reshape(-1, 1, DIFF_V_DIM)
    hd = MLSTM_HEAD_DIM

    bp, tp, _ = x_prompt.shape
    mk_p, mv_p = _memkv(mem_prompt.reshape(bp * N_MEM, D_MODEL), W["w_mem_kv"])
    mk_p4 = mk_p.reshape(DEPTH, bp, N_MEM, MEM_WIDTH)
    mv_p4 = mv_p.reshape(DEPTH, bp, N_MEM, MEM_WIDTH)
    ct0 = jnp.zeros((N_A_LAYERS, bp, MLSTM_HEADS, HEAD_PAD, HEAD_PAD), F32)
    m0 = jnp.zeros((N_A_LAYERS, bp, 8, LANES), F32)
    y_p, st_p, k_p, v_p = _trunk(x_prompt.reshape(bp * tp, D_MODEL), bp, mk_p4, mv_p4, ct0, m0, None, None,
                                 W, lambdas, g_sub, g_kv2)
    c_p, n_p, m_p = _unpack_states(st_p, bp)

    bs, ts, _ = x_sample.shape
    pad = HEAD_PAD - hd
    ct_s = jnp.concatenate([jnp.swapaxes(state_C, -1, -2), state_n[..., None],
                            jnp.zeros(state_n.shape + (pad - 1,), F32)], axis=-1)
    ct_s = jnp.pad(ct_s, [(0, 0)] * 3 + [(0, pad), (0, 0)])
    m_s = jnp.broadcast_to(jnp.pad(state_m, ((0, 0), (0, 0), (0, 8 - MLSTM_HEADS)))[..., None],
                           (N_A_LAYERS, bs, 8, LANES))
    y_s, st_s, k_s, v_s = _trunk(
        x_sample.reshape(bs * ts, D_MODEL), 1,
        cache_mem_k.reshape(DEPTH, bs, N_MEM, MEM_WIDTH), cache_mem_v.reshape(DEPTH, bs, N_MEM, MEM_WIDTH),
        ct_s, m_s, cache_k.reshape(bs, -1, MAIN_WIDTH), cache_v.reshape(bs, -1, MAIN_WIDTH),
        W, lambdas, g_sub, g_kv2)
    c_s, n_s, m_sn = _unpack_states(st_s, bs)

    kshape = (2 * DIFF_HEADS, DIFF_HEAD_DIM)
    vshape = (DIFF_HEADS, DIFF_V_DIM)
    mshape = (DEPTH, bp, N_MEM, MEM_HEADS, MEM_HEAD_DIM)
    return (y_p.reshape(bp, tp, D_MODEL), y_s.reshape(bs, ts, D_MODEL),
            mk_p.reshape(mshape), mv_p.reshape(mshape),
            c_p, n_p, m_p,
            k_p.reshape((bp, tp) + kshape), v_p.reshape((bp, tp) + vshape),
            c_s, n_s, m_sn,
            k_s.reshape((bs, ts) + kshape), v_s.reshape((bs, ts) + vshape))
```

```python
import functools
import math

import jax
import jax.numpy as jnp
from jax import lax
from jax.experimental import pallas as pl
from jax.experimental.pallas import tpu as pltpu

F32 = jnp.float32
BF16 = jnp.bfloat16

D_MODEL = 1024
DEPTH = 4
CHUNK = 64
N_MEM = 256
N_A_LAYERS = DEPTH // 2
MEM_HEADS = 4
MEM_WIDTH = 256
MEM_HEAD_DIM = 64
MAIN_WIDTH = 768
MLSTM_HEADS = 4
MLSTM_HEAD_DIM = 192
DIFF_HEAD_DIM = 64
DIFF_V_DIM = 128
DIFF_HEADS = 6
FFN_DIM = 2816
EPS = 1e-6

LANES = 128
HEAD_PAD = 256
N_COL = MLSTM_HEAD_DIM
A_WIDTH = MLSTM_HEADS * HEAD_PAD
QZ_WIDTH = 2 * DIFF_HEADS * LANES
MQZ_WIDTH = MEM_HEADS * LANES
LOG2E = math.log2(math.e)
VMEM_LIMIT = 56 * 1024 * 1024

TOKEN_TILE = 512
FFN_ROW_SPLIT = 2
MLSTM_TILE = 512
MLSTM_SEQ_CHUNK = 256
ATTN_TQ = 512
MEM_TILE = 512


def _params(*sem):
    return pltpu.CompilerParams(dimension_semantics=sem, vmem_limit_bytes=VMEM_LIMIT)


def _rms(x, g):
    return x * lax.rsqrt(jnp.mean(x * x, axis=-1, keepdims=True) + EPS) * g


def _log_sigmoid(x):
    return jnp.minimum(x, 0.0) - jnp.log1p(jnp.exp(-jnp.abs(x)))


def _dot(a, b):
    return jnp.dot(a, b, preferred_element_type=F32)


def _dot_nt(a, b):
    return lax.dot_general(a, b, (((1,), (1,)), ((), ())), preferred_element_type=F32)


def _resident(shape, index_map):
    return pl.BlockSpec(shape, index_map, pipeline_mode=pl.Buffered(1))


def _ffn_rows(x, g0_ref, wg_ref, wu_ref, wd_ref, g1_ref):
    xb = _rms(x, g0_ref[...]).astype(BF16)
    hg = _dot(xb, wg_ref[...])
    hu = _dot(xb, wu_ref[...])
    a = (hg * jax.nn.sigmoid(hg) * hu).astype(BF16)
    y = _dot(a, wd_ref[...])
    return x + 0.5 * _rms(y, g1_ref[...])


def _row_groups(tm):
    sub = tm // FFN_ROW_SPLIT
    return [slice(r * sub, (r + 1) * sub) for r in range(FFN_ROW_SPLIT)]


def _ffn_kernel(x_ref, g0_ref, wg_ref, wu_ref, wd_ref, g1_ref, o_ref):
    for rows in _row_groups(x_ref.shape[0]):
        o_ref[rows, :] = _ffn_rows(x_ref[rows, :], g0_ref, wg_ref, wu_ref, wd_ref, g1_ref)


def _mix_ffn_kernel(main_ref, mem_ref, x_ref, wm_ref, we_ref, g3_ref, g0_ref, wg_ref, wu_ref, wd_ref, g1_ref, o_ref):
    for rows in _row_groups(x_ref.shape[0]):
        y = _dot(main_ref[rows, :], wm_ref[...]) + _dot(mem_ref[rows, :], we_ref[...])
        x = x_ref[rows, :] + _rms(y, g3_ref[...])
        o_ref[rows, :] = _ffn_rows(x, g0_ref, wg_ref, wu_ref, wd_ref, g1_ref)


def _mix_ffn(main, mem, x, w_main, lm, w_mem, g3, g0, w_up, w_down, g1, l):
    n = x.shape[0]
    tm = min(TOKEN_TILE, n)
    wm = main.shape[1]
    row = lambda w: pl.BlockSpec((tm, w), lambda i: (i, 0))
    gain = pl.BlockSpec((1, D_MODEL), lambda i: (0, 0))
    return pl.pallas_call(
        _mix_ffn_kernel,
        out_shape=jax.ShapeDtypeStruct((n, D_MODEL), F32),
        grid=(n // tm,),
        in_specs=[
            row(wm), row(MEM_WIDTH), row(D_MODEL),
            _resident((None, wm, D_MODEL), lambda i: (lm, 0, 0)),
            _resident((None, MEM_WIDTH, D_MODEL), lambda i: (l, 0, 0)),
            gain, gain,
            _resident((None, None, D_MODEL, FFN_DIM), lambda i: (l, 1, 0, 0)),
            _resident((None, None, D_MODEL, FFN_DIM), lambda i: (l, 1, 0, 1)),
            _resident((None, None, FFN_DIM, D_MODEL), lambda i: (l, 1, 0, 0)),
            gain,
        ],
        out_specs=row(D_MODEL),
        compiler_params=_params("arbitrary"),
        name="mix_ffn",
    )(main, mem, x, w_main, w_mem, g3, g0, w_up, w_up, w_down, g1)


def _ffn(x, g0, w_up, w_down, g1, l, k):
    n = x.shape[0]
    tm = min(TOKEN_TILE, n)
    return pl.pallas_call(
        _ffn_kernel,
        out_shape=jax.ShapeDtypeStruct((n, D_MODEL), F32),
        grid=(n // tm,),
        in_specs=[
            pl.BlockSpec((tm, D_MODEL), lambda i: (i, 0)),
            pl.BlockSpec((1, D_MODEL), lambda i: (0, 0)),
            _resident((None, None, D_MODEL, FFN_DIM), lambda i: (l, k, 0, 0)),
            _resident((None, None, D_MODEL, FFN_DIM), lambda i: (l, k, 0, 1)),
            _resident((None, None, FFN_DIM, D_MODEL), lambda i: (l, k, 0, 0)),
            pl.BlockSpec((1, D_MODEL), lambda i: (0, 0)),
        ],
        out_specs=pl.BlockSpec((tm, D_MODEL), lambda i: (i, 0)),
        compiler_params=_params("arbitrary"),
        name="ffn",
    )(x, g0, w_up, w_up, w_down, g1)


def _inproj_a_kernel(x_ref, g_ref, wrow_ref, wkt_ref, wgt_ref, brow_ref, bcol_ref,
                     q_ref, kt_ref, v_ref, o_ref, mq_ref, gcol_ref, grow_ref):
    xb = _rms(x_ref[...], g_ref[...]).astype(BF16)
    y = _dot(xb, wrow_ref[...])
    q_ref[...] = y[:, 0:A_WIDTH].astype(BF16)
    v = y[:, A_WIDTH:2 * A_WIDTH]
    lane = lax.broadcasted_iota(jnp.int32, v.shape, 1)
    v_ref[...] = jnp.where(lane % HEAD_PAD == N_COL, 1.0, v).astype(BF16)
    o_ref[...] = y[:, 2 * A_WIDTH:3 * A_WIDTH]
    mq_ref[...] = (y[:, 3 * A_WIDTH:3 * A_WIDTH + MQZ_WIDTH] * (MEM_HEAD_DIM ** -0.5 * LOG2E)).astype(BF16)
    g = y[:, 3 * A_WIDTH + MQZ_WIDTH:] + brow_ref[...]
    glane = lax.broadcasted_iota(jnp.int32, g.shape, 1)
    gcol_ref[...] = jnp.where(glane < MLSTM_HEADS, g, _log_sigmoid(g))
    kt = _dot_nt(wkt_ref[...], xb)
    kt_ref[...] = (kt * (MLSTM_HEAD_DIM ** -0.5)).astype(BF16)
    gt = _dot_nt(wgt_ref[...], xb) + bcol_ref[...]
    grow_idx = lax.broadcasted_iota(jnp.int32, gt.shape, 0)
    grow_ref[...] = jnp.where(grow_idx < MLSTM_HEADS, gt, _log_sigmoid(gt))


def _inproj_a(x, g, wrow, wkt, wgt, brow, bcol, l, groups):
    n = x.shape[0]
    tg = n // groups
    tm = min(TOKEN_TILE, tg)
    per = tg // tm
    wcols = wrow.shape[2]
    outs = (
        jax.ShapeDtypeStruct((n, A_WIDTH), BF16),
        jax.ShapeDtypeStruct((groups, A_WIDTH, tg), BF16),
        jax.ShapeDtypeStruct((n, A_WIDTH), BF16),
        jax.ShapeDtypeStruct((n, A_WIDTH), F32),
        jax.ShapeDtypeStruct((n, MQZ_WIDTH), BF16),
        jax.ShapeDtypeStruct((n, LANES), F32),
        jax.ShapeDtypeStruct((groups, 8, tg), F32),
    )
    row = lambda w: pl.BlockSpec((tm, w), lambda i: (i, 0))
    return pl.pallas_call(
        _inproj_a_kernel,
        out_shape=outs,
        grid=(n // tm,),
        in_specs=[
            row(D_MODEL),
            pl.BlockSpec((1, D_MODEL), lambda i: (0, 0)),
            _resident((None, D_MODEL, wcols), lambda i: (l, 0, 0)),
            _resident((None, A_WIDTH, D_MODEL), lambda i: (l, 0, 0)),
            _resident((None, 8, D_MODEL), lambda i: (l, 0, 0)),
            pl.BlockSpec((None, 1, LANES), lambda i: (l, 0, 0)),
            pl.BlockSpec((None, 8, 1), lambda i: (l, 0, 0)),
        ],
        out_specs=(
            row(A_WIDTH),
            pl.BlockSpec((None, A_WIDTH, tm), lambda i: (i // per, 0, i % per)),
            row(A_WIDTH),
            row(A_WIDTH),
            row(MQZ_WIDTH),
            row(LANES),
            pl.BlockSpec((None, 8, tm), lambda i: (i // per, 0, i % per)),
        ),
        compiler_params=_params("arbitrary"),
        name="inproj_a",
    )(x, g, wrow, wkt, wgt, brow, bcol)


def _inproj_b_kernel(x_ref, g_ref, w_ref, qz_ref, mq_ref):
    xb = _rms(x_ref[...], g_ref[...]).astype(BF16)
    y = _dot(xb, w_ref[...])
    qz_ref[...] = (y[:, :QZ_WIDTH] * (DIFF_HEAD_DIM ** -0.5 * LOG2E)).astype(BF16)
    mq_ref[...] = (y[:, QZ_WIDTH:] * (MEM_HEAD_DIM ** -0.5 * LOG2E)).astype(BF16)


def _inproj_b(x, g, w, lb):
    n = x.shape[0]
    tm = min(TOKEN_TILE, n)
    return pl.pallas_call(
        _inproj_b_kernel,
        out_shape=(jax.ShapeDtypeStruct((n, QZ_WIDTH), BF16), jax.ShapeDtypeStruct((n, MQZ_WIDTH), BF16)),
        grid=(n // tm,),
        in_specs=[
            pl.BlockSpec((tm, D_MODEL), lambda i: (i, 0)),
            pl.BlockSpec((1, D_MODEL), lambda i: (0, 0)),
            _resident((None, D_MODEL, QZ_WIDTH + MQZ_WIDTH), lambda i: (lb, 0, 0)),
        ],
        out_specs=(pl.BlockSpec((tm, QZ_WIDTH), lambda i: (i, 0)), pl.BlockSpec((tm, MQZ_WIDTH), lambda i: (i, 0))),
        compiler_params=_params("arbitrary"),
        name="inproj_b",
    )(x, g, w)


def _kvproj_kernel(x_ref, g_ref, w_ref, k_ref, v_ref, kb_ref, vb_ref):
    xb = _rms(x_ref[...], g_ref[...]).astype(BF16)
    y = _dot(xb, w_ref[...])
    k = y[:, :MAIN_WIDTH]
    v = y[:, MAIN_WIDTH:]
    k_ref[...] = k
    v_ref[...] = v
    kb_ref[...] = k.astype(BF16)
    vb_ref[...] = v.astype(BF16)


def _kvproj(x, g, w):
    n = x.shape[0]
    tm = min(TOKEN_TILE, n)
    blk = pl.BlockSpec((tm, MAIN_WIDTH), lambda i: (i, 0))
    return pl.pallas_call(
        _kvproj_kernel,
        out_shape=(jax.ShapeDtypeStruct((n, MAIN_WIDTH), F32), jax.ShapeDtypeStruct((n, MAIN_WIDTH), F32),
                   jax.ShapeDtypeStruct((n, MAIN_WIDTH), BF16), jax.ShapeDtypeStruct((n, MAIN_WIDTH), BF16)),
        grid=(n // tm,),
        in_specs=[
            pl.BlockSpec((tm, D_MODEL), lambda i: (i, 0)),
            pl.BlockSpec((1, D_MODEL), lambda i: (0, 0)),
            _resident((D_MODEL, 2 * MAIN_WIDTH), lambda i: (0, 0)),
        ],
        out_specs=(blk, blk, blk, blk),
        compiler_params=_params("arbitrary"),
        name="kvproj",
    )(x, g, w)


def _kvproj_seq_kernel(x_ref, g_ref, w_ref, wkt_ref, kt_ref, v4_ref, kb_ref, vb_ref):
    xb = _rms(x_ref[...], g_ref[...]).astype(BF16)
    y = _dot(xb, w_ref[...])
    v = y[:, MAIN_WIDTH:]
    kb_ref[...] = y[:, :MAIN_WIDTH].astype(BF16)
    vb_ref[...] = v.astype(BF16)
    kt_ref[...] = _dot_nt(wkt_ref[...], xb)
    for h in range(DIFF_HEADS):
        v4_ref[h] = v[:, h * DIFF_V_DIM:(h + 1) * DIFF_V_DIM]


def _kvproj_seq(x, g, w, wkt, groups):
    n = x.shape[0]
    tg = n // groups
    tm = min(TOKEN_TILE, tg)
    per = tg // tm
    blk = pl.BlockSpec((tm, MAIN_WIDTH), lambda i: (i, 0))
    return pl.pallas_call(
        _kvproj_seq_kernel,
        out_shape=(jax.ShapeDtypeStruct((groups, MAIN_WIDTH, tg), F32),
                   jax.ShapeDtypeStruct((groups, DIFF_HEADS, tg, DIFF_V_DIM), F32),
                   jax.ShapeDtypeStruct((n, MAIN_WIDTH), BF16), jax.ShapeDtypeStruct((n, MAIN_WIDTH), BF16)),
        grid=(n // tm,),
        in_specs=[
            pl.BlockSpec((tm, D_MODEL), lambda i: (i, 0)),
            pl.BlockSpec((1, D_MODEL), lambda i: (0, 0)),
            _resident((D_MODEL, 2 * MAIN_WIDTH), lambda i: (0, 0)),
            _resident((MAIN_WIDTH, D_MODEL), lambda i: (0, 0)),
        ],
        out_specs=(pl.BlockSpec((None, MAIN_WIDTH, tm), lambda i: (i // per, 0, i % per)),
                   pl.BlockSpec((None, DIFF_HEADS, tm, DIFF_V_DIM), lambda i: (i // per, 0, i % per, 0)),
                   blk, blk),
        compiler_params=_params("arbitrary"),
        name="kvproj_seq",
    )(x, g, w, wkt)


def _memkv_kernel(m_ref, wt_ref, kt_ref, vt_ref):
    yt = _dot_nt(wt_ref[...], m_ref[...].astype(BF16))
    kt_ref[...] = yt[:MEM_WIDTH]
    vt_ref[...] = yt[MEM_WIDTH:]


def _memkv(mem, wt):
    b = mem.shape[0]
    out = jax.ShapeDtypeStruct((DEPTH, b, MEM_WIDTH, N_MEM), F32)
    blk = pl.BlockSpec((None, None, MEM_WIDTH, N_MEM), lambda l, i: (l, i, 0, 0))
    return pl.pallas_call(
        _memkv_kernel,
        out_shape=(out, out),
        grid=(DEPTH, b),
        in_specs=[pl.BlockSpec((None, N_MEM, D_MODEL), lambda l, i: (i, 0, 0)),
                  pl.BlockSpec((None, 2 * MEM_WIDTH, D_MODEL), lambda l, i: (l, 0, 0))],
        out_specs=(blk, blk),
        compiler_params=_params("arbitrary", "arbitrary"),
        name="memkv",
    )(mem, wt)


def _split3(x):
    hi = x.astype(BF16)
    r = x - hi.astype(F32)
    mid = r.astype(BF16)
    lo = (r - mid.astype(F32)).astype(BF16)
    return hi, mid, lo


def _mlstm_kernel(q_ref, kt_ref, v_ref, o_ref, gcol_ref, grow_ref, gh_ref, ct0_ref, m0_ref,
                  h_ref, ctn_ref, mn_ref, ct_s, m_s, tri_s, *, sequential, n_chunks, chunk):
    t = pl.program_id(1)
    tb = n_chunks * chunk
    win = max(chunk, LANES)

    @pl.when((pl.program_id(0) == 0) & (t == 0))
    def _init_tri():
        r = lax.broadcasted_iota(jnp.int32, (tb, tb), 0)
        c = lax.broadcasted_iota(jnp.int32, (tb, tb), 1)
        tri_s[...] = jnp.where((r // chunk == c // chunk) & (c <= r), 1.0, 0.0).astype(BF16)

    @pl.when((t == 0) | (not sequential))
    def _init_state():
        ct_s[...] = ct0_ref[...]
        m_s[...] = m0_ref[...]

    tri = tri_s[...]
    gcol = gcol_ref[...]
    grow = grow_ref[...]
    bcol_all = sum(_dot(tri, part) for part in _split3(gcol))
    brow_all = sum(_dot_nt(part, tri) for part in _split3(grow))

    row_i = lax.broadcasted_iota(jnp.int32, (chunk, win), 0)
    lane_i = lax.broadcasted_iota(jnp.int32, (chunk, win), 1)
    lane_row = lax.broadcasted_iota(jnp.int32, (1, win), 1)
    pad_lane = lax.broadcasted_iota(jnp.int32, (chunk, HEAD_PAD), 1)

    for c in range(n_chunks):
        slot = 0 if sequential else c
        r0 = c * chunk
        p0 = r0 // win * win
        off = r0 - p0
        in_chunk = (lane_i >= off) & (lane_i < off + chunk)
        vis = in_chunk & (lane_i - off <= row_i)
        in_chunk_row = (lane_row >= off) & (lane_row < off + chunk)
        for h in range(MLSTM_HEADS):
            hs = slice(h * HEAD_PAD, (h + 1) * HEAD_PAD)
            fl = MLSTM_HEADS + h
            bcol = bcol_all[r0:r0 + chunk, fl:fl + 1]
            brow = brow_all[fl:fl + 1, p0:p0 + win]
            igrow = grow[h:h + 1, p0:p0 + win]
            m_prev = m_s[slot, h:h + 1, 0:1]
            dlog = jnp.where(vis, bcol - brow + igrow, -jnp.inf)
            a = jnp.max(dlog, axis=1, keepdims=True)
            mt = jnp.maximum(bcol + m_prev, a)
            e = jnp.exp(dlog - mt)
            w_inter = jnp.exp(bcol + m_prev - mt)
            qh = q_ref[r0:r0 + chunk, hs]
            kth = kt_ref[hs, p0:p0 + win]
            vh = v_ref[p0:p0 + win, hs]
            s = (_dot(qh, kth) * e).astype(BF16)
            ct = ct_s[slot, h]
            num = w_inter * _dot(qh, ct.astype(BF16)) + _dot(s, vh)
            nq = num[:, N_COL:N_COL + 1]
            den = jnp.maximum(jnp.abs(nq), jnp.exp(-mt))
            hh = jnp.where(pad_lane < MLSTM_HEAD_DIM, num / den, 0.0)
            ms = jnp.sum(hh * hh, axis=1, keepdims=True) * (1.0 / MLSTM_HEAD_DIM)
            hn = hh * lax.rsqrt(ms + EPS) * gh_ref[:, hs]
            h_ref[r0:r0 + chunk, hs] = (hn * jax.nn.sigmoid(o_ref[r0:r0 + chunk, hs])).astype(BF16)
            last = off + chunk - 1
            b_last = brow[:, last:last + 1]
            m_new = mt[chunk - 1:chunk, :]
            w_end = jnp.where(in_chunk_row, jnp.exp(b_last - brow + igrow - m_new), 0.0)
            decay = jnp.exp(b_last + m_prev - m_new)
            kw = (kth.astype(F32) * w_end).astype(BF16)
            ct_s[slot, h] = decay * ct + _dot(kw, vh)
            m_s[slot, h:h + 1, :] = jnp.broadcast_to(m_new, (1, LANES))

    @pl.when((t == pl.num_programs(1) - 1) | (not sequential))
    def _fin():
        ctn_ref[...] = ct_s[...]
        mn_ref[...] = m_s[...]


def _mlstm(q, kt, v, o, gcol, grow, ghead, ct0, m0, l, groups, sequential):
    n = q.shape[0]
    tg = n // groups
    tb = min(MLSTM_TILE, tg)
    chunk = min(MLSTM_SEQ_CHUNK, tb) if sequential else CHUNK
    n_chunks = tb // chunk
    per = tg // tb
    slots = 1 if sequential else n_chunks
    streams = ct0.shape[1]
    sblk = (lambda g, t: g) if sequential else (lambda g, t: g * per + t)
    row = lambda w: pl.BlockSpec((tb, w), lambda g, t: (g * per + t, 0))
    kernel = functools.partial(_mlstm_kernel, sequential=sequential, n_chunks=n_chunks, chunk=chunk)
    return pl.pallas_call(
        kernel,
        out_shape=(jax.ShapeDtypeStruct((n, A_WIDTH), BF16),
                   jax.ShapeDtypeStruct((streams, MLSTM_HEADS, HEAD_PAD, HEAD_PAD), F32),
                   jax.ShapeDtypeStruct((streams, 8, LANES), F32)),
        grid=(groups, per),
        in_specs=[
            row(A_WIDTH),
            pl.BlockSpec((None, A_WIDTH, tb), lambda g, t: (g, 0, t)),
            row(A_WIDTH),
            row(A_WIDTH),
            row(LANES),
            pl.BlockSpec((None, 8, tb), lambda g, t: (g, 0, t)),
            pl.BlockSpec((None, 1, A_WIDTH), lambda g, t: (l, 0, 0)),
            pl.BlockSpec((None, slots, MLSTM_HEADS, HEAD_PAD, HEAD_PAD), lambda g, t: (l, sblk(g, t), 0, 0, 0)),
            pl.BlockSpec((None, slots, 8, LANES), lambda g, t: (l, sblk(g, t), 0, 0)),
        ],
        out_specs=(
            row(A_WIDTH),
            pl.BlockSpec((slots, MLSTM_HEADS, HEAD_PAD, HEAD_PAD), lambda g, t: (sblk(g, t), 0, 0, 0)),
            pl.BlockSpec((slots, 8, LANES), lambda g, t: (sblk(g, t), 0, 0)),
        ),
        scratch_shapes=[
            pltpu.VMEM((slots, MLSTM_HEADS, HEAD_PAD, HEAD_PAD), F32),
            pltpu.VMEM((slots, 8, LANES), F32),
            pltpu.VMEM((tb, tb), BF16),
        ],
        compiler_params=_params("arbitrary", "arbitrary"),
        name="mlstm",
    )(q, kt, v, o, gcol, grow, ghead, ct0, m0)


_SLOPES2 = tuple(LOG2E * 2.0 ** (-8.0 * (h + 1) / DIFF_HEADS) for h in range(DIFF_HEADS))


def _lam(lam_ref, lam_init):
    lp = lam_ref[...]
    s1 = jnp.sum(lp[0:1] * lp[1:2], axis=1, keepdims=True)
    s2 = jnp.sum(lp[2:3] * lp[3:4], axis=1, keepdims=True)
    return jnp.exp(s1) - jnp.exp(s2) + lam_init


def _vext(v_tile):
    return jnp.concatenate([v_tile, jnp.ones_like(v_tile)], axis=1)


def _finish_head(acc1, acc2, lam, gsub, lam_init):
    o = acc1[:, :DIFF_V_DIM] / acc1[:, DIFF_V_DIM:] - lam * (acc2[:, :DIFF_V_DIM] / acc2[:, DIFF_V_DIM:])
    return (_rms(o, gsub) * (1.0 - lam_init)).astype(BF16)


def _attn_prompt_kernel(it_ref, jt_ref, qz_ref, k_ref, v_ref, lam_ref, gsub_ref, o_ref, acc_s, m_s, *, lam_init, tq):
    step_id = pl.program_id(1)
    i = it_ref[step_id]
    j = jt_ref[step_id]
    tk = 2 * tq
    last = j == (i >> 1)
    odd = (i & 1) == 1

    @pl.when(j == 0)
    def _init():
        acc_s[...] = jnp.zeros_like(acc_s)
        m_s[...] = jnp.full_like(m_s, -jnp.inf)

    def step(keys, bias_fn):
        for a in range(2):
            for h in range(DIFF_HEADS):
                idx = a * DIFF_HEADS + h
                qt = qz_ref[:, idx * LANES:(idx + 1) * LANES]
                kt_i = a * (DIFF_HEADS // 2) + h // 2
                kt = k_ref[0:keys, kt_i * LANES:(kt_i + 1) * LANES]
                s = bias_fn(_dot_nt(qt, kt), h)
                m_prev = m_s[idx]
                m_new = jnp.maximum(m_prev, jnp.max(s, axis=1, keepdims=True))
                p = jnp.exp2(s - jnp.concatenate([m_new] * (keys // LANES), axis=1)).astype(BF16)
                alpha = jnp.exp2(m_prev - m_new)
                vt = _vext(v_ref[0:keys, h * LANES:(h + 1) * LANES])
                acc_s[idx] = jnp.concatenate([alpha, alpha], axis=1) * acc_s[idx] + _dot(p, vt)
                m_s[idx] = m_new

    def finish():
        lam = _lam(lam_ref, lam_init)
        for h in range(DIFF_HEADS):
            o_ref[:, h * LANES:(h + 1) * LANES] = _finish_head(
                acc_s[h], acc_s[DIFF_HEADS + h], lam, gsub_ref[...], lam_init)

    def own_keys(keys):
        r = lax.broadcasted_iota(jnp.int32, (tq, keys), 0)
        c = lax.broadcasted_iota(jnp.int32, (tq, keys), 1) - (keys - tq)
        vis = (c < 0) | (jnp.maximum(c, 0) // CHUNK <= r // CHUNK)
        g = (r - jnp.abs(r - c)).astype(F32)
        step(keys, lambda s, h: jnp.where(vis, s + _SLOPES2[h] * g, -jnp.inf))
        finish()

    @pl.when(jnp.logical_not(last))
    def _past():
        col = lax.broadcasted_iota(jnp.int32, (1, tk), 1).astype(F32)
        off = (j * tk - i * tq).astype(F32)
        step(tk, lambda s, h: s + _SLOPES2[h] * (col + off))

    @pl.when(last & odd)
    def _diag_odd():
        own_keys(tk)

    @pl.when(last & jnp.logical_not(odd))
    def _diag_even():
        own_keys(tq)


def _attn_prompt(qz, kb, vb, lambdas, gsub, lam_init, groups):
    n = qz.shape[0]
    tg = n // groups
    tq = min(ATTN_TQ, tg // 2)
    tk = 2 * tq
    nq = tg // tq
    nkv = tg // tk
    steps = [(i, j) for i in range(nq) for j in range(i // 2 + 1)]
    itab = jnp.asarray([s[0] for s in steps], jnp.int32)
    jtab = jnp.asarray([s[1] for s in steps], jnp.int32)
    kernel = functools.partial(_attn_prompt_kernel, lam_init=lam_init, tq=tq)
    q_spec = lambda w: pl.BlockSpec((tq, w), lambda b, s, it, jt: (b * nq + it[s], 0))
    kv_spec = pl.BlockSpec((tk, MAIN_WIDTH), lambda b, s, it, jt: (b * nkv + jt[s], 0))
    grid_spec = pltpu.PrefetchScalarGridSpec(
        num_scalar_prefetch=2,
        grid=(groups, len(steps)),
        in_specs=[
            q_spec(QZ_WIDTH),
            kv_spec,
            kv_spec,
            pl.BlockSpec((4, DIFF_HEAD_DIM), lambda b, s, it, jt: (0, 0)),
            pl.BlockSpec((1, DIFF_V_DIM), lambda b, s, it, jt: (0, 0)),
        ],
        out_specs=q_spec(MAIN_WIDTH),
        scratch_shapes=[
            pltpu.VMEM((2 * DIFF_HEADS, tq, 2 * DIFF_V_DIM), F32),
            pltpu.VMEM((2 * DIFF_HEADS, tq, LANES), F32),
        ],
    )
    return pl.pallas_call(
        kernel,
        out_shape=jax.ShapeDtypeStruct((n, MAIN_WIDTH), BF16),
        grid_spec=grid_spec,
        compiler_params=_params("arbitrary", "arbitrary"),
        name="attn_prompt",
    )(itab, jtab, qz, kb, vb, lambdas, gsub)


def _attn_sample_kernel(qz_ref, ckt_ref, cv_ref, nk_ref, nv_ref, lam_ref, gsub_ref, o_ref, *, lam_init, past):
    tq = qz_ref.shape[0]

    def geometry(width, c0):
        r = lax.broadcasted_iota(jnp.int32, (tq, width), 0) + past
        c = lax.broadcasted_iota(jnp.int32, (tq, width), 1) + c0
        return c // CHUNK <= r // CHUNK, jnp.abs(r - c).astype(F32)

    vis_c, dist_c = geometry(past, 0)
    vis_n, dist_n = geometry(tq, past)
    lam = _lam(lam_ref, lam_init)
    for h in range(DIFF_HEADS):
        accs = []
        for a in range(2):
            idx = a * DIFF_HEADS + h
            qt = qz_ref[:, idx * LANES:(idx + 1) * LANES]
            kt_i = a * (DIFF_HEADS // 2) + h // 2
            ks = slice(kt_i * LANES, (kt_i + 1) * LANES)
            s_c = jnp.where(vis_c, _dot(qt, ckt_ref[ks, :].astype(BF16)) - _SLOPES2[h] * dist_c, -jnp.inf)
            s_n = jnp.where(vis_n, _dot_nt(qt, nk_ref[:, ks]) - _SLOPES2[h] * dist_n, -jnp.inf)
            m = jnp.maximum(jnp.max(s_c, axis=1, keepdims=True), jnp.max(s_n, axis=1, keepdims=True))
            accs.append(_dot(jnp.exp2(s_c - m).astype(BF16), _vext(cv_ref[h].astype(BF16)))
                        + _dot(jnp.exp2(s_n - m).astype(BF16), _vext(nv_ref[:, h * LANES:(h + 1) * LANES])))
        o_ref[:, h * LANES:(h + 1) * LANES] = _finish_head(accs[0], accs[1], lam, gsub_ref[...], lam_init)


def _attn_sample(qz, cache_kt, cache_v, kb, vb, lambdas, gsub, lam_init):
    streams, _, past = cache_kt.shape
    n = qz.shape[0]
    tq = n // streams
    kernel = functools.partial(_attn_sample_kernel, lam_init=lam_init, past=past)
    new = pl.BlockSpec((tq, MAIN_WIDTH), lambda b: (b, 0))
    return pl.pallas_call(
        kernel,
        out_shape=jax.ShapeDtypeStruct((n, MAIN_WIDTH), BF16),
        grid=(streams,),
        in_specs=[
            pl.BlockSpec((tq, QZ_WIDTH), lambda b: (b, 0)),
            pl.BlockSpec((None, MAIN_WIDTH, past), lambda b: (b, 0, 0)),
            pl.BlockSpec((None, DIFF_HEADS, past, DIFF_V_DIM), lambda b: (b, 0, 0, 0)),
            new, new,
            pl.BlockSpec((4, DIFF_HEAD_DIM), lambda b: (0, 0)),
            pl.BlockSpec((1, DIFF_V_DIM), lambda b: (0, 0)),
        ],
        out_specs=new,
        compiler_params=_params("arbitrary"),
        name="attn_sample",
    )(qz, cache_kt, cache_v, kb, vb, lambdas, gsub)


def _memattn_kernel(mq_ref, mkt_ref, mvt_ref, o_ref, *, tq):
    lane = lax.broadcasted_iota(jnp.int32, (tq, LANES), 1)
    for i in range(mkt_ref.shape[0]):
        rows = slice(i * tq, (i + 1) * tq)
        for pair in range(MEM_HEADS // 2):
            ks = slice(pair * LANES, (pair + 1) * LANES)
            mkt = mkt_ref[i, ks, :].astype(BF16)
            mvt = mvt_ref[i, ks, :].astype(BF16)
            outs = []
            for h in (2 * pair, 2 * pair + 1):
                s = _dot(mq_ref[rows, h * LANES:(h + 1) * LANES], mkt)
                p = jnp.exp2(s - jnp.max(s, axis=1, keepdims=True))
                l = jnp.sum(p, axis=1, keepdims=True)
                outs.append(_dot_nt(p.astype(BF16), mvt) / l)
            o_ref[rows, ks] = jnp.where(lane < MEM_HEAD_DIM, outs[0], outs[1]).astype(BF16)


def _memattn(mq, mkt, mvt, l):
    n = mq.shape[0]
    streams = mkt.shape[1]
    tg = n // streams
    tq = min(MEM_TILE, tg)
    per = tg // tq
    ns = MEM_TILE // tq
    rows = ns * tq
    mem = pl.BlockSpec((None, ns, MEM_WIDTH, N_MEM), lambda i: (l, i // per, 0, 0))
    return pl.pallas_call(
        functools.partial(_memattn_kernel, tq=tq),
        out_shape=jax.ShapeDtypeStruct((n, MEM_WIDTH), BF16),
        grid=(n // rows,),
        in_specs=[pl.BlockSpec((rows, MQZ_WIDTH), lambda i: (i, 0)), mem, mem],
        out_specs=pl.BlockSpec((rows, MEM_WIDTH), lambda i: (i, 0)),
        compiler_params=_params("arbitrary"),
        name="memattn",
    )(mq, mkt, mvt)


def _pad_heads(w, heads, dim, pad):
    lead = w.shape[:-1]
    w = w.reshape(lead + (heads, dim))
    w = jnp.pad(w, [(0, 0)] * len(lead) + [(0, 0), (0, pad - dim)])
    return w.reshape(lead + (heads * pad,))


def _interleave64(w, heads):
    lead = w.shape[:-1]
    w = w.reshape(lead + (heads, 64))
    z = jnp.zeros_like(w)
    even = jnp.concatenate([w, z], axis=-1)
    odd = jnp.concatenate([z, w], axis=-1)
    sel = (jnp.arange(heads) % 2 == 0)[:, None]
    return jnp.where(sel, even, odd).reshape(lead + (heads * LANES,))


def _prep_weights(w_up, w_down, norm_g, w_in_a, b_gates, g_head_a, w_in_b, w_kv, w_mem_kv, w_out):
    mw = MAIN_WIDTH
    prep = {
        "w_up": w_up.astype(BF16),
        "w_down": w_down.astype(BF16),
        "norm_g": norm_g.reshape(DEPTH, 6, 1, D_MODEL),
        "w_kv": w_kv.astype(BF16),
        "w_kt": w_kv[:, :mw].T.astype(BF16),
        "w_mem_kv_t": jnp.swapaxes(w_mem_kv, 1, 2).astype(BF16),
        "w_out_mem": w_out[:, mw:, :].astype(BF16),
        "w_out_b": w_out[N_A_LAYERS:, :mw, :].astype(BF16),
    }
    wa = w_out[:N_A_LAYERS, :mw, :].reshape(N_A_LAYERS, MLSTM_HEADS, MLSTM_HEAD_DIM, D_MODEL)
    wa = jnp.pad(wa, ((0, 0), (0, 0), (0, HEAD_PAD - MLSTM_HEAD_DIM), (0, 0)))
    prep["w_out_a"] = wa.reshape(N_A_LAYERS, A_WIDTH, D_MODEL).astype(BF16)
    pad_a = lambda w: _pad_heads(w, MLSTM_HEADS, MLSTM_HEAD_DIM, HEAD_PAD)
    q, k, v, o = (pad_a(w_in_a[:, :, s * mw:(s + 1) * mw]) for s in range(4))
    gates = w_in_a[:, :, 4 * mw:4 * mw + 2 * MLSTM_HEADS]
    mq = _interleave64(w_in_a[:, :, 4 * mw + 2 * MLSTM_HEADS:], MEM_HEADS)
    gates_pad = jnp.pad(gates, ((0, 0), (0, 0), (0, LANES - 2 * MLSTM_HEADS)))
    prep["wrow_a"] = jnp.concatenate([q, v, o, mq, gates_pad], axis=-1).astype(BF16)
    prep["wkt_a"] = jnp.swapaxes(k, 1, 2).astype(BF16)
    prep["wgt_a"] = jnp.swapaxes(gates, 1, 2).astype(BF16)
    prep["brow_a"] = jnp.pad(b_gates, ((0, 0), (0, LANES - 2 * MLSTM_HEADS))).reshape(N_A_LAYERS, 1, LANES)
    prep["bcol_a"] = b_gates.reshape(N_A_LAYERS, 2 * MLSTM_HEADS, 1)
    prep["ghead_a"] = jnp.pad(g_head_a, ((0, 0), (0, 0), (0, HEAD_PAD - MLSTM_HEAD_DIM))).reshape(N_A_LAYERS, 1, A_WIDTH)
    nq = 2 * DIFF_HEADS * DIFF_HEAD_DIM
    prep["w_in_b"] = jnp.concatenate(
        [_interleave64(w_in_b[:, :, :nq], 2 * DIFF_HEADS), _interleave64(w_in_b[:, :, nq:], MEM_HEADS)],
        axis=-1).astype(BF16)
    return prep


def _trunk(x, groups, mem_k, mem_v, ct0, m0, cache_k, cache_v, W, lambdas, g_sub_b, g_kv):
    prompt = cache_k is None
    g = W["norm_g"]
    states = []
    k_new = v_new = kb = vb = None
    for l in range(DEPTH):
        x = _ffn(x, g[l, 0], W["w_up"], W["w_down"], g[l, 1], l, 0)
        if l < N_A_LAYERS:
            q, kt, v, o, mq, gcol, grow = _inproj_a(
                x, g[l, 2], W["wrow_a"], W["wkt_a"], W["wgt_a"], W["brow_a"], W["bcol_a"], l, groups)
            main, ctn, mn = _mlstm(q, kt, v, o, gcol, grow, W["ghead_a"], ct0, m0, l, groups, prompt)
            states.append((ctn, mn))
            w_main, lm = W["w_out_a"], l
        else:
            lb = l - N_A_LAYERS
            lam_init = 0.8 - 0.6 * math.exp(-0.3 * l)
            qz, mq = _inproj_b(x, g[l, 2], W["w_in_b"], lb)
            if prompt:
                main = _attn_prompt(qz, kb, vb, lambdas[lb], g_sub_b[lb], lam_init, groups)
            else:
                main = _attn_sample(qz, cache_k, cache_v, kb, vb, lambdas[lb], g_sub_b[lb], lam_init)
            w_main, lm = W["w_out_b"], lb
        mem = _memattn(mq, mem_k, mem_v, l)
        x = _mix_ffn(main, mem, x, w_main, lm, W["w_out_mem"], g[l, 3], g[l, 4], W["w_up"], W["w_down"], g[l, 5], l)
        if l == N_A_LAYERS - 1:
            if prompt:
                k_new, v_new, kb, vb = _kvproj_seq(x, g_kv, W["w_kv"], W["w_kt"], groups)
            else:
                k_new, v_new, kb, vb = _kvproj(x, g_kv, W["w_kv"])
    return x, states, k_new, v_new


def _unpack_states(states, lead):
    hd = MLSTM_HEAD_DIM
    ct = jnp.stack([s[0] for s in states])
    m = jnp.stack([s[1] for s in states])
    c = jnp.swapaxes(ct[..., :hd, :hd], -1, -2)
    n = ct[..., :hd, N_COL]
    return c, n, m[..., :MLSTM_HEADS, 0]


def kernel(x_prompt, x_sample, mem_prompt, cache_mem_k, cache_mem_v, state_C, state_n, state_m, cache_k, cache_v,
           w_up, w_down, norm_g, w_in_a, b_gates, g_head_a, w_in_b, lambdas, g_sub_b, g_kv, w_kv, w_mem_kv, w_out):
    W = _prep_weights(w_up, w_down, norm_g, w_in_a, b_gates, g_head_a, w_in_b, w_kv, w_mem_kv, w_out)
    g_kv2 = g_kv.reshape(1, D_MODEL)
    g_sub = g_sub_b.reshape(-1, 1, DIFF_V_DIM)
    hd = MLSTM_HEAD_DIM

    bp, tp, _ = x_prompt.shape
    mkt_p, mvt_p = _memkv(mem_prompt, W["w_mem_kv_t"])
    ct0 = jnp.zeros((N_A_LAYERS, bp, MLSTM_HEADS, HEAD_PAD, HEAD_PAD), F32)
    m0 = jnp.zeros((N_A_LAYERS, bp, 8, LANES), F32)
    y_p, st_p, kt_p, v4_p = _trunk(x_prompt.reshape(bp * tp, D_MODEL), bp, mkt_p, mvt_p, ct0, m0, None, None,
                                   W, lambdas, g_sub, g_kv2)
    c_p, n_p, m_p = _unpack_states(st_p, bp)

    bs, ts, _ = x_sample.shape
    pad = HEAD_PAD - hd
    ct_s = jnp.concatenate([jnp.swapaxes(state_C, -1, -2), state_n[..., None],
                            jnp.zeros(state_n.shape + (pad - 1,), F32)], axis=-1)
    ct_s = jnp.pad(ct_s, [(0, 0)] * 3 + [(0, pad), (0, 0)])
    m_s = jnp.broadcast_to(jnp.pad(state_m, ((0, 0), (0, 0), (0, 8 - MLSTM_HEADS)))[..., None],
                           (N_A_LAYERS, bs, 8, LANES))
    mem_t = lambda a: jnp.transpose(a, (0, 1, 3, 4, 2)).reshape(DEPTH, -1, MEM_WIDTH, N_MEM)
    y_s, st_s, k_s, v_s = _trunk(
        x_sample.reshape(bs * ts, D_MODEL), 1, mem_t(cache_mem_k), mem_t(cache_mem_v), ct_s, m_s,
        jnp.transpose(cache_k, (0, 2, 3, 1)).reshape(bs, MAIN_WIDTH, -1), jnp.transpose(cache_v, (0, 2, 1, 3)),
        W, lambdas, g_sub, g_kv2)
    c_s, n_s, m_sn = _unpack_states(st_s, bs)

    kshape = (2 * DIFF_HEADS, DIFF_HEAD_DIM)
    vshape = (DIFF_HEADS, DIFF_V_DIM)
    mem_out = lambda a: jnp.transpose(a.reshape(DEPTH, bp, MEM_HEADS, MEM_HEAD_DIM, N_MEM), (0, 1, 4, 2, 3))
    return (y_p.reshape(bp, tp, D_MODEL), y_s.reshape(bs, ts, D_MODEL),
            mem_out(mkt_p), mem_out(mvt_p),
            c_p, n_p, m_p,
            jnp.transpose(kt_p.reshape((bp,) + kshape + (tp,)), (0, 3, 1, 2)), jnp.transpose(v4_p, (0, 2, 1, 3)),
            c_s, n_s, m_sn,
            k_s.reshape((bs, ts) + kshape), v_s.reshape((bs, ts) + vshape))
```

```python
import functools
import math

import jax
import jax.numpy as jnp
from jax import lax
from jax.experimental import pallas as pl
from jax.experimental.pallas import tpu as pltpu

F32 = jnp.float32
BF16 = jnp.bfloat16

D_MODEL = 1024
DEPTH = 4
CHUNK = 64
N_MEM = 256
N_A_LAYERS = DEPTH // 2
MEM_HEADS = 4
MEM_WIDTH = 256
MEM_HEAD_DIM = 64
MAIN_WIDTH = 768
MLSTM_HEADS = 4
MLSTM_HEAD_DIM = 192
DIFF_HEAD_DIM = 64
DIFF_V_DIM = 128
DIFF_HEADS = 6
FFN_DIM = 2816
EPS = 1e-6

LANES = 128
HEAD_PAD = 256
N_COL = MLSTM_HEAD_DIM
A_WIDTH = MLSTM_HEADS * HEAD_PAD
QZ_WIDTH = 2 * DIFF_HEADS * LANES
MQZ_WIDTH = MEM_HEADS * LANES
LOG2E = math.log2(math.e)
VMEM_LIMIT = 56 * 1024 * 1024

TOKEN_TILE = 512
FFN_ROW_SPLIT = 2
MLSTM_TILE = 512
MLSTM_SEQ_CHUNK = 256
MLSTM_GROUPS_PER_STEP = 4
ATTN_TQ = 512
MEM_TILE = 512


def _params(*sem):
    return pltpu.CompilerParams(dimension_semantics=sem, vmem_limit_bytes=VMEM_LIMIT)


def _rms(x, g):
    return x * lax.rsqrt(jnp.mean(x * x, axis=-1, keepdims=True) + EPS) * g


def _log_sigmoid(x):
    return jnp.minimum(x, 0.0) - jnp.log1p(jnp.exp(-jnp.abs(x)))


def _dot(a, b):
    return jnp.dot(a, b, preferred_element_type=F32)


def _dot_nt(a, b):
    return lax.dot_general(a, b, (((1,), (1,)), ((), ())), preferred_element_type=F32)


def _resident(shape, index_map):
    return pl.BlockSpec(shape, index_map, pipeline_mode=pl.Buffered(1))


def _ffn_rows(x, g0_ref, wg_ref, wu_ref, wd_ref, g1_ref):
    xb = _rms(x, g0_ref[...]).astype(BF16)
    hg = _dot(xb, wg_ref[...])
    hu = _dot(xb, wu_ref[...])
    a = (hg * jax.nn.sigmoid(hg) * hu).astype(BF16)
    y = _dot(a, wd_ref[...])
    return x + 0.5 * _rms(y, g1_ref[...])


def _row_groups(tm):
    sub = tm // FFN_ROW_SPLIT
    return [slice(r * sub, (r + 1) * sub) for r in range(FFN_ROW_SPLIT)]


def _ffn_kernel(x_ref, g0_ref, wg_ref, wu_ref, wd_ref, g1_ref, o_ref):
    for rows in _row_groups(x_ref.shape[0]):
        o_ref[rows, :] = _ffn_rows(x_ref[rows, :], g0_ref, wg_ref, wu_ref, wd_ref, g1_ref)


def _mix_ffn_kernel(main_ref, mem_ref, x_ref, wm_ref, we_ref, g3_ref, g0_ref, wg_ref, wu_ref, wd_ref, g1_ref, o_ref):
    for rows in _row_groups(x_ref.shape[0]):
        y = _dot(main_ref[rows, :], wm_ref[...]) + _dot(mem_ref[rows, :], we_ref[...])
        x = x_ref[rows, :] + _rms(y, g3_ref[...])
        o_ref[rows, :] = _ffn_rows(x, g0_ref, wg_ref, wu_ref, wd_ref, g1_ref)


def _mix_ffn(main, mem, x, w_main, lm, w_mem, g3, g0, w_up, w_down, g1, l):
    n = x.shape[0]
    tm = min(TOKEN_TILE, n)
    wm = main.shape[1]
    row = lambda w: pl.BlockSpec((tm, w), lambda i: (i, 0))
    gain = pl.BlockSpec((1, D_MODEL), lambda i: (0, 0))
    return pl.pallas_call(
        _mix_ffn_kernel,
        out_shape=jax.ShapeDtypeStruct((n, D_MODEL), F32),
        grid=(n // tm,),
        in_specs=[
            row(wm), row(MEM_WIDTH), row(D_MODEL),
            _resident((None, wm, D_MODEL), lambda i: (lm, 0, 0)),
            _resident((None, MEM_WIDTH, D_MODEL), lambda i: (l, 0, 0)),
            gain, gain,
            _resident((None, None, D_MODEL, FFN_DIM), lambda i: (l, 1, 0, 0)),
            _resident((None, None, D_MODEL, FFN_DIM), lambda i: (l, 1, 0, 1)),
            _resident((None, None, FFN_DIM, D_MODEL), lambda i: (l, 1, 0, 0)),
            gain,
        ],
        out_specs=row(D_MODEL),
        compiler_params=_params("arbitrary"),
        name="mix_ffn",
    )(main, mem, x, w_main, w_mem, g3, g0, w_up, w_up, w_down, g1)


def _ffn(x, g0, w_up, w_down, g1, l, k):
    n = x.shape[0]
    tm = min(TOKEN_TILE, n)
    return pl.pallas_call(
        _ffn_kernel,
        out_shape=jax.ShapeDtypeStruct((n, D_MODEL), F32),
        grid=(n // tm,),
        in_specs=[
            pl.BlockSpec((tm, D_MODEL), lambda i: (i, 0)),
            pl.BlockSpec((1, D_MODEL), lambda i: (0, 0)),
            _resident((None, None, D_MODEL, FFN_DIM), lambda i: (l, k, 0, 0)),
            _resident((None, None, D_MODEL, FFN_DIM), lambda i: (l, k, 0, 1)),
            _resident((None, None, FFN_DIM, D_MODEL), lambda i: (l, k, 0, 0)),
            pl.BlockSpec((1, D_MODEL), lambda i: (0, 0)),
        ],
        out_specs=pl.BlockSpec((tm, D_MODEL), lambda i: (i, 0)),
        compiler_params=_params("arbitrary"),
        name="ffn",
    )(x, g0, w_up, w_up, w_down, g1)


def _inproj_a_kernel(x_ref, g_ref, wrow_ref, wkt_ref, wgt_ref, brow_ref, bcol_ref,
                     q_ref, kt_ref, v_ref, o_ref, mq_ref, gcol_ref, grow_ref):
    xb = _rms(x_ref[...], g_ref[...]).astype(BF16)
    y = _dot(xb, wrow_ref[...])
    q_ref[...] = y[:, 0:A_WIDTH].astype(BF16)
    v = y[:, A_WIDTH:2 * A_WIDTH]
    lane = lax.broadcasted_iota(jnp.int32, v.shape, 1)
    v_ref[...] = jnp.where(lane % HEAD_PAD == N_COL, 1.0, v).astype(BF16)
    o_ref[...] = y[:, 2 * A_WIDTH:3 * A_WIDTH]
    mq_ref[...] = (y[:, 3 * A_WIDTH:3 * A_WIDTH + MQZ_WIDTH] * (MEM_HEAD_DIM ** -0.5 * LOG2E)).astype(BF16)
    g = y[:, 3 * A_WIDTH + MQZ_WIDTH:] + brow_ref[...]
    glane = lax.broadcasted_iota(jnp.int32, g.shape, 1)
    gcol_ref[...] = jnp.where(glane < MLSTM_HEADS, g, _log_sigmoid(g))
    kt = _dot_nt(wkt_ref[...], xb)
    kt_ref[...] = (kt * (MLSTM_HEAD_DIM ** -0.5)).astype(BF16)
    gt = _dot_nt(wgt_ref[...], xb) + bcol_ref[...]
    grow_idx = lax.broadcasted_iota(jnp.int32, gt.shape, 0)
    grow_ref[...] = jnp.where(grow_idx < MLSTM_HEADS, gt, _log_sigmoid(gt))


def _inproj_a(x, g, wrow, wkt, wgt, brow, bcol, l, groups):
    n = x.shape[0]
    tg = n // groups
    tm = min(TOKEN_TILE, tg)
    per = tg // tm
    wcols = wrow.shape[2]
    outs = (
        jax.ShapeDtypeStruct((n, A_WIDTH), BF16),
        jax.ShapeDtypeStruct((groups, A_WIDTH, tg), BF16),
        jax.ShapeDtypeStruct((n, A_WIDTH), BF16),
        jax.ShapeDtypeStruct((n, A_WIDTH), F32),
        jax.ShapeDtypeStruct((n, MQZ_WIDTH), BF16),
        jax.ShapeDtypeStruct((n, LANES), F32),
        jax.ShapeDtypeStruct((groups, 8, tg), F32),
    )
    row = lambda w: pl.BlockSpec((tm, w), lambda i: (i, 0))
    return pl.pallas_call(
        _inproj_a_kernel,
        out_shape=outs,
        grid=(n // tm,),
        in_specs=[
            row(D_MODEL),
            pl.BlockSpec((1, D_MODEL), lambda i: (0, 0)),
            _resident((None, D_MODEL, wcols), lambda i: (l, 0, 0)),
            _resident((None, A_WIDTH, D_MODEL), lambda i: (l, 0, 0)),
            _resident((None, 8, D_MODEL), lambda i: (l, 0, 0)),
            pl.BlockSpec((None, 1, LANES), lambda i: (l, 0, 0)),
            pl.BlockSpec((None, 8, 1), lambda i: (l, 0, 0)),
        ],
        out_specs=(
            row(A_WIDTH),
            pl.BlockSpec((None, A_WIDTH, tm), lambda i: (i // per, 0, i % per)),
            row(A_WIDTH),
            row(A_WIDTH),
            row(MQZ_WIDTH),
            row(LANES),
            pl.BlockSpec((None, 8, tm), lambda i: (i // per, 0, i % per)),
        ),
        compiler_params=_params("arbitrary"),
        name="inproj_a",
    )(x, g, wrow, wkt, wgt, brow, bcol)


def _inproj_b_kernel(x_ref, g_ref, w_ref, qz_ref, mq_ref):
    xb = _rms(x_ref[...], g_ref[...]).astype(BF16)
    y = _dot(xb, w_ref[...])
    qz_ref[...] = (y[:, :QZ_WIDTH] * (DIFF_HEAD_DIM ** -0.5 * LOG2E)).astype(BF16)
    mq_ref[...] = (y[:, QZ_WIDTH:] * (MEM_HEAD_DIM ** -0.5 * LOG2E)).astype(BF16)


def _inproj_b(x, g, w, lb):
    n = x.shape[0]
    tm = min(TOKEN_TILE, n)
    return pl.pallas_call(
        _inproj_b_kernel,
        out_shape=(jax.ShapeDtypeStruct((n, QZ_WIDTH), BF16), jax.ShapeDtypeStruct((n, MQZ_WIDTH), BF16)),
        grid=(n // tm,),
        in_specs=[
            pl.BlockSpec((tm, D_MODEL), lambda i: (i, 0)),
            pl.BlockSpec((1, D_MODEL), lambda i: (0, 0)),
            _resident((None, D_MODEL, QZ_WIDTH + MQZ_WIDTH), lambda i: (lb, 0, 0)),
        ],
        out_specs=(pl.BlockSpec((tm, QZ_WIDTH), lambda i: (i, 0)), pl.BlockSpec((tm, MQZ_WIDTH), lambda i: (i, 0))),
        compiler_params=_params("arbitrary"),
        name="inproj_b",
    )(x, g, w)


def _kvproj_kernel(x_ref, g_ref, w_ref, k_ref, v_ref, kb_ref, vb_ref):
    xb = _rms(x_ref[...], g_ref[...]).astype(BF16)
    y = _dot(xb, w_ref[...])
    k = y[:, :MAIN_WIDTH]
    v = y[:, MAIN_WIDTH:]
    k_ref[...] = k
    v_ref[...] = v
    kb_ref[...] = k.astype(BF16)
    vb_ref[...] = v.astype(BF16)


def _kvproj(x, g, w):
    n = x.shape[0]
    tm = min(TOKEN_TILE, n)
    blk = pl.BlockSpec((tm, MAIN_WIDTH), lambda i: (i, 0))
    return pl.pallas_call(
        _kvproj_kernel,
        out_shape=(jax.ShapeDtypeStruct((n, MAIN_WIDTH), F32), jax.ShapeDtypeStruct((n, MAIN_WIDTH), F32),
                   jax.ShapeDtypeStruct((n, MAIN_WIDTH), BF16), jax.ShapeDtypeStruct((n, MAIN_WIDTH), BF16)),
        grid=(n // tm,),
        in_specs=[
            pl.BlockSpec((tm, D_MODEL), lambda i: (i, 0)),
            pl.BlockSpec((1, D_MODEL), lambda i: (0, 0)),
            _resident((D_MODEL, 2 * MAIN_WIDTH), lambda i: (0, 0)),
        ],
        out_specs=(blk, blk, blk, blk),
        compiler_params=_params("arbitrary"),
        name="kvproj",
    )(x, g, w)


def _kvproj_seq_kernel(x_ref, g_ref, w_ref, wkt_ref, kt_ref, v4_ref, kb_ref, vb_ref):
    xb = _rms(x_ref[...], g_ref[...]).astype(BF16)
    y = _dot(xb, w_ref[...])
    v = y[:, MAIN_WIDTH:]
    kb_ref[...] = y[:, :MAIN_WIDTH].astype(BF16)
    vb_ref[...] = v.astype(BF16)
    kt_ref[...] = _dot_nt(wkt_ref[...], xb)
    for h in range(DIFF_HEADS):
        v4_ref[h] = v[:, h * DIFF_V_DIM:(h + 1) * DIFF_V_DIM]


def _kvproj_seq(x, g, w, wkt, groups):
    n = x.shape[0]
    tg = n // groups
    tm = min(TOKEN_TILE, tg)
    per = tg // tm
    blk = pl.BlockSpec((tm, MAIN_WIDTH), lambda i: (i, 0))
    return pl.pallas_call(
        _kvproj_seq_kernel,
        out_shape=(jax.ShapeDtypeStruct((groups, MAIN_WIDTH, tg), F32),
                   jax.ShapeDtypeStruct((groups, DIFF_HEADS, tg, DIFF_V_DIM), F32),
                   jax.ShapeDtypeStruct((n, MAIN_WIDTH), BF16), jax.ShapeDtypeStruct((n, MAIN_WIDTH), BF16)),
        grid=(n // tm,),
        in_specs=[
            pl.BlockSpec((tm, D_MODEL), lambda i: (i, 0)),
            pl.BlockSpec((1, D_MODEL), lambda i: (0, 0)),
            _resident((D_MODEL, 2 * MAIN_WIDTH), lambda i: (0, 0)),
            _resident((MAIN_WIDTH, D_MODEL), lambda i: (0, 0)),
        ],
        out_specs=(pl.BlockSpec((None, MAIN_WIDTH, tm), lambda i: (i // per, 0, i % per)),
                   pl.BlockSpec((None, DIFF_HEADS, tm, DIFF_V_DIM), lambda i: (i // per, 0, i % per, 0)),
                   blk, blk),
        compiler_params=_params("arbitrary"),
        name="kvproj_seq",
    )(x, g, w, wkt)


def _memkv_kernel(m_ref, wt_ref, kt_ref, vt_ref):
    yt = _dot_nt(wt_ref[...], m_ref[...].astype(BF16))
    kt_ref[...] = yt[:MEM_WIDTH]
    vt_ref[...] = yt[MEM_WIDTH:]


def _memkv(mem, wt):
    b = mem.shape[0]
    out = jax.ShapeDtypeStruct((DEPTH, b, MEM_WIDTH, N_MEM), F32)
    blk = pl.BlockSpec((None, None, MEM_WIDTH, N_MEM), lambda l, i: (l, i, 0, 0))
    return pl.pallas_call(
        _memkv_kernel,
        out_shape=(out, out),
        grid=(DEPTH, b),
        in_specs=[pl.BlockSpec((None, N_MEM, D_MODEL), lambda l, i: (i, 0, 0)),
                  pl.BlockSpec((None, 2 * MEM_WIDTH, D_MODEL), lambda l, i: (l, 0, 0))],
        out_specs=(blk, blk),
        compiler_params=_params("arbitrary", "arbitrary"),
        name="memkv",
    )(mem, wt)


def _split3(x):
    hi = x.astype(BF16)
    r = x - hi.astype(F32)
    mid = r.astype(BF16)
    lo = (r - mid.astype(F32)).astype(BF16)
    return hi, mid, lo


def _mlstm_kernel(q_ref, kt_ref, v_ref, o_ref, gcol_ref, grow_ref, gh_ref, ct0_ref, m0_ref,
                  h_ref, ctn_ref, mn_ref, ct_s, m_s, tri_s, *, sequential, chunk):
    t = pl.program_id(1)
    nb, tb = q_ref.shape[0], q_ref.shape[1]
    win = max(chunk, LANES)
    units = [(b, c * chunk) for b in range(nb) for c in range(tb // chunk)]

    @pl.when((pl.program_id(0) == 0) & (t == 0))
    def _init_tri():
        r = lax.broadcasted_iota(jnp.int32, (tb, tb), 0)
        c = lax.broadcasted_iota(jnp.int32, (tb, tb), 1)
        tri_s[...] = jnp.where((r // chunk == c // chunk) & (c <= r), 1.0, 0.0).astype(BF16)

    @pl.when((t == 0) | (not sequential))
    def _init_state():
        ct_s[...] = ct0_ref[...]
        m_s[...] = m0_ref[...]

    tri = tri_s[...]
    bcols = [sum(_dot(tri, part) for part in _split3(gcol_ref[b])) for b in range(nb)]
    brows = [sum(_dot_nt(part, tri) for part in _split3(grow_ref[b])) for b in range(nb)]

    row_i = lax.broadcasted_iota(jnp.int32, (chunk, win), 0)
    lane_i = lax.broadcasted_iota(jnp.int32, (chunk, win), 1)
    lane_row = lax.broadcasted_iota(jnp.int32, (1, win), 1)
    pad_lane = lax.broadcasted_iota(jnp.int32, (chunk, HEAD_PAD), 1)

    pairs = []
    for slot, (b, r0) in enumerate(units):
        p0 = r0 // win * win
        off = r0 - p0
        in_chunk = (lane_i >= off) & (lane_i < off + chunk)
        vis = in_chunk & (lane_i - off <= row_i)
        in_chunk_row = (lane_row >= off) & (lane_row < off + chunk)
        for h in range(MLSTM_HEADS):
            pairs.append(dict(slot=slot, b=b, r0=r0, p0=p0, off=off, h=h, vis=vis, in_chunk_row=in_chunk_row,
                              hs=slice(h * HEAD_PAD, (h + 1) * HEAD_PAD)))

    for p in pairs:
        fl = MLSTM_HEADS + p["h"]
        bcol = bcols[p["b"]][p["r0"]:p["r0"] + chunk, fl:fl + 1]
        p["brow"] = brows[p["b"]][fl:fl + 1, p["p0"]:p["p0"] + win]
        p["igrow"] = grow_ref[p["b"], p["h"]:p["h"] + 1, p["p0"]:p["p0"] + win]
        p["m_prev"] = m_s[p["slot"], p["h"]:p["h"] + 1, 0:1]
        dlog = jnp.where(p["vis"], bcol - p["brow"] + p["igrow"], -jnp.inf)
        a = jnp.max(dlog, axis=1, keepdims=True)
        p["mt"] = jnp.maximum(bcol + p["m_prev"], a)
        p["e"] = jnp.exp(dlog - p["mt"])
        p["w_inter"] = jnp.exp(bcol + p["m_prev"] - p["mt"])

    for p in pairs:
        b, r0, p0, hs = p["b"], p["r0"], p["p0"], p["hs"]
        qh = q_ref[b, r0:r0 + chunk, hs]
        p["kth"] = kt_ref[b, hs, p0:p0 + win]
        p["vh"] = v_ref[b, p0:p0 + win, hs]
        s = (_dot(qh, p["kth"]) * p["e"]).astype(BF16)
        p["ct"] = ct_s[p["slot"], p["h"]]
        p["num"] = p["w_inter"] * _dot(qh, p["ct"].astype(BF16)) + _dot(s, p["vh"])

    for p in pairs:
        b, r0, hs, num = p["b"], p["r0"], p["hs"], p["num"]
        nq = num[:, N_COL:N_COL + 1]
        den = jnp.maximum(jnp.abs(nq), jnp.exp(-p["mt"]))
        hh = jnp.where(pad_lane < MLSTM_HEAD_DIM, num / den, 0.0)
        ms = jnp.sum(hh * hh, axis=1, keepdims=True) * (1.0 / MLSTM_HEAD_DIM)
        hn = hh * lax.rsqrt(ms + EPS) * gh_ref[:, hs]
        h_ref[b, r0:r0 + chunk, hs] = (hn * jax.nn.sigmoid(o_ref[b, r0:r0 + chunk, hs])).astype(BF16)

    for p in pairs:
        last = p["off"] + chunk - 1
        b_last = p["brow"][:, last:last + 1]
        m_new = p["mt"][chunk - 1:chunk, :]
        w_end = jnp.where(p["in_chunk_row"], jnp.exp(b_last - p["brow"] + p["igrow"] - m_new), 0.0)
        decay = jnp.exp(b_last + p["m_prev"] - m_new)
        kw = (p["kth"].astype(F32) * w_end).astype(BF16)
        ct_s[p["slot"], p["h"]] = decay * p["ct"] + _dot(kw, p["vh"])
        m_s[p["slot"], p["h"]:p["h"] + 1, :] = jnp.broadcast_to(m_new, (1, LANES))

    @pl.when((t == pl.num_programs(1) - 1) | (not sequential))
    def _fin():
        ctn_ref[...] = ct_s[...]
        mn_ref[...] = m_s[...]


def _mlstm(q, kt, v, o, gcol, grow, ghead, ct0, m0, l, groups, sequential):
    n = q.shape[0]
    tg = n // groups
    if sequential:
        chunk = tb = min(MLSTM_SEQ_CHUNK, tg)
        nb = min(MLSTM_GROUPS_PER_STEP, groups)
    else:
        chunk, tb, nb = CHUNK, min(MLSTM_TILE, tg), 1
    steps = tg // tb
    slots = nb * (tb // chunk)
    streams = ct0.shape[1]
    sblk = (lambda g, t: g) if sequential else (lambda g, t: g * steps + t)
    rows3 = lambda a: a.reshape(groups, tg, a.shape[-1])
    row = lambda w: pl.BlockSpec((nb, tb, w), lambda g, t: (g, t, 0))
    col = lambda r: pl.BlockSpec((nb, r, tb), lambda g, t: (g, 0, t))
    kernel = functools.partial(_mlstm_kernel, sequential=sequential, chunk=chunk)
    h, ctn, mn = pl.pallas_call(
        kernel,
        out_shape=(jax.ShapeDtypeStruct((groups, tg, A_WIDTH), BF16),
                   jax.ShapeDtypeStruct((streams, MLSTM_HEADS, HEAD_PAD, HEAD_PAD), F32),
                   jax.ShapeDtypeStruct((streams, 8, LANES), F32)),
        grid=(groups // nb, steps),
        in_specs=[
            row(A_WIDTH),
            col(A_WIDTH),
            row(A_WIDTH),
            row(A_WIDTH),
            row(LANES),
            col(8),
            pl.BlockSpec((None, 1, A_WIDTH), lambda g, t: (l, 0, 0)),
            pl.BlockSpec((None, slots, MLSTM_HEADS, HEAD_PAD, HEAD_PAD), lambda g, t: (l, sblk(g, t), 0, 0, 0)),
            pl.BlockSpec((None, slots, 8, LANES), lambda g, t: (l, sblk(g, t), 0, 0)),
        ],
        out_specs=(
            row(A_WIDTH),
            pl.BlockSpec((slots, MLSTM_HEADS, HEAD_PAD, HEAD_PAD), lambda g, t: (sblk(g, t), 0, 0, 0)),
            pl.BlockSpec((slots, 8, LANES), lambda g, t: (sblk(g, t), 0, 0)),
        ),
        scratch_shapes=[
            pltpu.VMEM((slots, MLSTM_HEADS, HEAD_PAD, HEAD_PAD), F32),
            pltpu.VMEM((slots, 8, LANES), F32),
            pltpu.VMEM((tb, tb), BF16),
        ],
        compiler_params=_params("arbitrary", "arbitrary"),
        name="mlstm",
    )(rows3(q), kt, rows3(v), rows3(o), rows3(gcol), grow, ghead, ct0, m0)
    return h.reshape(n, A_WIDTH), ctn, mn


_SLOPES2 = tuple(LOG2E * 2.0 ** (-8.0 * (h + 1) / DIFF_HEADS) for h in range(DIFF_HEADS))


def _lam(lam_ref, lam_init):
    lp = lam_ref[...]
    s1 = jnp.sum(lp[0:1] * lp[1:2], axis=1, keepdims=True)
    s2 = jnp.sum(lp[2:3] * lp[3:4], axis=1, keepdims=True)
    return jnp.exp(s1) - jnp.exp(s2) + lam_init


def _vext(v_tile):
    return jnp.concatenate([v_tile, jnp.ones_like(v_tile)], axis=1)


def _finish_head(acc1, acc2, lam, gsub, lam_init):
    o = acc1[:, :DIFF_V_DIM] / acc1[:, DIFF_V_DIM:] - lam * (acc2[:, :DIFF_V_DIM] / acc2[:, DIFF_V_DIM:])
    return (_rms(o, gsub) * (1.0 - lam_init)).astype(BF16)


def _attn_prompt_kernel(it_ref, jt_ref, qz_ref, k_ref, v_ref, lam_ref, gsub_ref, o_ref, acc_s, m_s, *, lam_init, tq):
    step_id = pl.program_id(1)
    i = it_ref[step_id]
    j = jt_ref[step_id]
    tk = 2 * tq
    last = j == (i >> 1)
    odd = (i & 1) == 1

    @pl.when(j == 0)
    def _init():
        acc_s[...] = jnp.zeros_like(acc_s)
        m_s[...] = jnp.full_like(m_s, -jnp.inf)

    def step(keys, bias_fn):
        def scores(idx):
            h = idx % DIFF_HEADS
            qt = qz_ref[:, idx * LANES:(idx + 1) * LANES]
            kt_i = idx // DIFF_HEADS * (DIFF_HEADS // 2) + h // 2
            kt = k_ref[0:keys, kt_i * LANES:(kt_i + 1) * LANES]
            return bias_fn(_dot_nt(qt, kt), h)

        s_next = scores(0)
        for idx in range(2 * DIFF_HEADS):
            s = s_next
            if idx + 1 < 2 * DIFF_HEADS:
                s_next = scores(idx + 1)
            h = idx % DIFF_HEADS
            m_prev = m_s[idx]
            m_new = jnp.maximum(m_prev, jnp.max(s, axis=1, keepdims=True))
            p = jnp.exp2(s - jnp.concatenate([m_new] * (keys // LANES), axis=1)).astype(BF16)
            alpha = jnp.exp2(m_prev - m_new)
            vt = _vext(v_ref[0:keys, h * LANES:(h + 1) * LANES])
            acc_s[idx] = jnp.concatenate([alpha, alpha], axis=1) * acc_s[idx] + _dot(p, vt)
            m_s[idx] = m_new

    def finish():
        lam = _lam(lam_ref, lam_init)
        for h in range(DIFF_HEADS):
            o_ref[:, h * LANES:(h + 1) * LANES] = _finish_head(
                acc_s[h], acc_s[DIFF_HEADS + h], lam, gsub_ref[...], lam_init)

    def own_keys(keys):
        r = lax.broadcasted_iota(jnp.int32, (tq, keys), 0)
        c = lax.broadcasted_iota(jnp.int32, (tq, keys), 1) - (keys - tq)
        vis = (c < 0) | (jnp.maximum(c, 0) // CHUNK <= r // CHUNK)
        g = (r - jnp.abs(r - c)).astype(F32)
        step(keys, lambda s, h: jnp.where(vis, s + _SLOPES2[h] * g, -jnp.inf))
        finish()

    @pl.when(jnp.logical_not(last))
    def _past():
        col = lax.broadcasted_iota(jnp.int32, (1, tk), 1).astype(F32)
        off = (j * tk - i * tq).astype(F32)
        step(tk, lambda s, h: s + _SLOPES2[h] * (col + off))

    @pl.when(last & odd)
    def _diag_odd():
        own_keys(tk)

    @pl.when(last & jnp.logical_not(odd))
    def _diag_even():
        own_keys(tq)


def _attn_prompt(qz, kb, vb, lambdas, gsub, lam_init, groups):
    n = qz.shape[0]
    tg = n // groups
    tq = min(ATTN_TQ, tg // 2)
    tk = 2 * tq
    nq = tg // tq
    nkv = tg // tk
    steps = [(i, j) for i in range(nq) for j in range(i // 2 + 1)]
    itab = jnp.asarray([s[0] for s in steps], jnp.int32)
    jtab = jnp.asarray([s[1] for s in steps], jnp.int32)
    kernel = functools.partial(_attn_prompt_kernel, lam_init=lam_init, tq=tq)
    q_spec = lambda w: pl.BlockSpec((tq, w), lambda b, s, it, jt: (b * nq + it[s], 0))
    kv_spec = pl.BlockSpec((tk, MAIN_WIDTH), lambda b, s, it, jt: (b * nkv + jt[s], 0))
    grid_spec = pltpu.PrefetchScalarGridSpec(
        num_scalar_prefetch=2,
        grid=(groups, len(steps)),
        in_specs=[
            q_spec(QZ_WIDTH),
            kv_spec,
            kv_spec,
            pl.BlockSpec((4, DIFF_HEAD_DIM), lambda b, s, it, jt: (0, 0)),
            pl.BlockSpec((1, DIFF_V_DIM), lambda b, s, it, jt: (0, 0)),
        ],
        out_specs=q_spec(MAIN_WIDTH),
        scratch_shapes=[
            pltpu.VMEM((2 * DIFF_HEADS, tq, 2 * DIFF_V_DIM), F32),
            pltpu.VMEM((2 * DIFF_HEADS, tq, LANES), F32),
        ],
    )
    return pl.pallas_call(
        kernel,
        out_shape=jax.ShapeDtypeStruct((n, MAIN_WIDTH), BF16),
        grid_spec=grid_spec,
        compiler_params=_params("arbitrary", "arbitrary"),
        name="attn_prompt",
    )(itab, jtab, qz, kb, vb, lambdas, gsub)


def _attn_sample_kernel(qz_ref, ckt_ref, cv_ref, nk_ref, nv_ref, lam_ref, gsub_ref, o_ref, *, lam_init, past):
    tq = qz_ref.shape[0]

    def geometry(width, c0):
        r = lax.broadcasted_iota(jnp.int32, (tq, width), 0) + past
        c = lax.broadcasted_iota(jnp.int32, (tq, width), 1) + c0
        return c // CHUNK <= r // CHUNK, jnp.abs(r - c).astype(F32)

    vis_c, dist_c = geometry(past, 0)
    vis_n, dist_n = geometry(tq, past)
    lam = _lam(lam_ref, lam_init)
    for h in range(DIFF_HEADS):
        accs = []
        for a in range(2):
            idx = a * DIFF_HEADS + h
            qt = qz_ref[:, idx * LANES:(idx + 1) * LANES]
            kt_i = a * (DIFF_HEADS // 2) + h // 2
            ks = slice(kt_i * LANES, (kt_i + 1) * LANES)
            s_c = jnp.where(vis_c, _dot(qt, ckt_ref[ks, :].astype(BF16)) - _SLOPES2[h] * dist_c, -jnp.inf)
            s_n = jnp.where(vis_n, _dot_nt(qt, nk_ref[:, ks]) - _SLOPES2[h] * dist_n, -jnp.inf)
            m = jnp.maximum(jnp.max(s_c, axis=1, keepdims=True), jnp.max(s_n, axis=1, keepdims=True))
            accs.append(_dot(jnp.exp2(s_c - m).astype(BF16), _vext(cv_ref[h].astype(BF16)))
                        + _dot(jnp.exp2(s_n - m).astype(BF16), _vext(nv_ref[:, h * LANES:(h + 1) * LANES])))
        o_ref[:, h * LANES:(h + 1) * LANES] = _finish_head(accs[0], accs[1], lam, gsub_ref[...], lam_init)


def _attn_sample(qz, cache_kt, cache_v, kb, vb, lambdas, gsub, lam_init):
    streams, _, past = cache_kt.shape
    n = qz.shape[0]
    tq = n // streams
    kernel = functools.partial(_attn_sample_kernel, lam_init=lam_init, past=past)
    new = pl.BlockSpec((tq, MAIN_WIDTH), lambda b: (b, 0))
    return pl.pallas_call(
        kernel,
        out_shape=jax.ShapeDtypeStruct((n, MAIN_WIDTH), BF16),
        grid=(streams,),
        in_specs=[
            pl.BlockSpec((tq, QZ_WIDTH), lambda b: (b, 0)),
            pl.BlockSpec((None, MAIN_WIDTH, past), lambda b: (b, 0, 0)),
            pl.BlockSpec((None, DIFF_HEADS, past, DIFF_V_DIM), lambda b: (b, 0, 0, 0)),
            new, new,
            pl.BlockSpec((4, DIFF_HEAD_DIM), lambda b: (0, 0)),
            pl.BlockSpec((1, DIFF_V_DIM), lambda b: (0, 0)),
        ],
        out_specs=new,
        compiler_params=_params("arbitrary"),
        name="attn_sample",
    )(qz, cache_kt, cache_v, kb, vb, lambdas, gsub)


def _memattn_kernel(mq_ref, mkt_ref, mvt_ref, o_ref, *, tq):
    lane = lax.broadcasted_iota(jnp.int32, (tq, LANES), 1)
    for i in range(mkt_ref.shape[0]):
        rows = slice(i * tq, (i + 1) * tq)
        for pair in range(MEM_HEADS // 2):
            ks = slice(pair * LANES, (pair + 1) * LANES)
            mkt = mkt_ref[i, ks, :].astype(BF16)
            mvt = mvt_ref[i, ks, :].astype(BF16)
            outs = []
            for h in (2 * pair, 2 * pair + 1):
                s = _dot(mq_ref[rows, h * LANES:(h + 1) * LANES], mkt)
                p = jnp.exp2(s - jnp.max(s, axis=1, keepdims=True))
                l = jnp.sum(p, axis=1, keepdims=True)
                outs.append(_dot_nt(p.astype(BF16), mvt) / l)
            o_ref[rows, ks] = jnp.where(lane < MEM_HEAD_DIM, outs[0], outs[1]).astype(BF16)


def _memattn(mq, mkt, mvt, l):
    n = mq.shape[0]
    streams = mkt.shape[1]
    tg = n // streams
    tq = min(MEM_TILE, tg)
    per = tg // tq
    ns = MEM_TILE // tq
    rows = ns * tq
    mem = pl.BlockSpec((None, ns, MEM_WIDTH, N_MEM), lambda i: (l, i // per, 0, 0))
    return pl.pallas_call(
        functools.partial(_memattn_kernel, tq=tq),
        out_shape=jax.ShapeDtypeStruct((n, MEM_WIDTH), BF16),
        grid=(n // rows,),
        in_specs=[pl.BlockSpec((rows, MQZ_WIDTH), lambda i: (i, 0)), mem, mem],
        out_specs=pl.BlockSpec((rows, MEM_WIDTH), lambda i: (i, 0)),
        compiler_params=_params("arbitrary"),
        name="memattn",
    )(mq, mkt, mvt)


def _pad_heads(w, heads, dim, pad):
    lead = w.shape[:-1]
    w = w.reshape(lead + (heads, dim))
    w = jnp.pad(w, [(0, 0)] * len(lead) + [(0, 0), (0, pad - dim)])
    return w.reshape(lead + (heads * pad,))


def _interleave64(w, heads):
    lead = w.shape[:-1]
    w = w.reshape(lead + (heads, 64))
    z = jnp.zeros_like(w)
    even = jnp.concatenate([w, z], axis=-1)
    odd = jnp.concatenate([z, w], axis=-1)
    sel = (jnp.arange(heads) % 2 == 0)[:, None]
    return jnp.where(sel, even, odd).reshape(lead + (heads * LANES,))


def _prep_weights(w_up, w_down, norm_g, w_in_a, b_gates, g_head_a, w_in_b, w_kv, w_mem_kv, w_out):
    mw = MAIN_WIDTH
    prep = {
        "w_up": w_up.astype(BF16),
        "w_down": w_down.astype(BF16),
        "norm_g": norm_g.reshape(DEPTH, 6, 1, D_MODEL),
        "w_kv": w_kv.astype(BF16),
        "w_kt": w_kv[:, :mw].T.astype(BF16),
        "w_mem_kv_t": jnp.swapaxes(w_mem_kv, 1, 2).astype(BF16),
        "w_out_mem": w_out[:, mw:, :].astype(BF16),
        "w_out_b": w_out[N_A_LAYERS:, :mw, :].astype(BF16),
    }
    wa = w_out[:N_A_LAYERS, :mw, :].reshape(N_A_LAYERS, MLSTM_HEADS, MLSTM_HEAD_DIM, D_MODEL)
    wa = jnp.pad(wa, ((0, 0), (0, 0), (0, HEAD_PAD - MLSTM_HEAD_DIM), (0, 0)))
    prep["w_out_a"] = wa.reshape(N_A_LAYERS, A_WIDTH, D_MODEL).astype(BF16)
    pad_a = lambda w: _pad_heads(w, MLSTM_HEADS, MLSTM_HEAD_DIM, HEAD_PAD)
    q, k, v, o = (pad_a(w_in_a[:, :, s * mw:(s + 1) * mw]) for s in range(4))
    gates = w_in_a[:, :, 4 * mw:4 * mw + 2 * MLSTM_HEADS]
    mq = _interleave64(w_in_a[:, :, 4 * mw + 2 * MLSTM_HEADS:], MEM_HEADS)
    gates_pad = jnp.pad(gates, ((0, 0), (0, 0), (0, LANES - 2 * MLSTM_HEADS)))
    prep["wrow_a"] = jnp.concatenate([q, v, o, mq, gates_pad], axis=-1).astype(BF16)
    prep["wkt_a"] = jnp.swapaxes(k, 1, 2).astype(BF16)
    prep["wgt_a"] = jnp.swapaxes(gates, 1, 2).astype(BF16)
    prep["brow_a"] = jnp.pad(b_gates, ((0, 0), (0, LANES - 2 * MLSTM_HEADS))).reshape(N_A_LAYERS, 1, LANES)
    prep["bcol_a"] = b_gates.reshape(N_A_LAYERS, 2 * MLSTM_HEADS, 1)
    prep["ghead_a"] = jnp.pad(g_head_a, ((0, 0), (0, 0), (0, HEAD_PAD - MLSTM_HEAD_DIM))).reshape(N_A_LAYERS, 1, A_WIDTH)
    nq = 2 * DIFF_HEADS * DIFF_HEAD_DIM
    prep["w_in_b"] = jnp.concatenate(
        [_interleave64(w_in_b[:, :, :nq], 2 * DIFF_HEADS), _interleave64(w_in_b[:, :, nq:], MEM_HEADS)],
        axis=-1).astype(BF16)
    return prep


def _trunk(x, groups, mem_k, mem_v, ct0, m0, cache_k, cache_v, W, lambdas, g_sub_b, g_kv):
    prompt = cache_k is None
    g = W["norm_g"]
    states = []
    k_new = v_new = kb = vb = None
    for l in range(DEPTH):
        x = _ffn(x, g[l, 0], W["w_up"], W["w_down"], g[l, 1], l, 0)
        if l < N_A_LAYERS:
            q, kt, v, o, mq, gcol, grow = _inproj_a(
                x, g[l, 2], W["wrow_a"], W["wkt_a"], W["wgt_a"], W["brow_a"], W["bcol_a"], l, groups)
            main, ctn, mn = _mlstm(q, kt, v, o, gcol, grow, W["ghead_a"], ct0, m0, l, groups, prompt)
            states.append((ctn, mn))
            w_main, lm = W["w_out_a"], l
        else:
            lb = l - N_A_LAYERS
            lam_init = 0.8 - 0.6 * math.exp(-0.3 * l)
            qz, mq = _inproj_b(x, g[l, 2], W["w_in_b"], lb)
            if prompt:
                main = _attn_prompt(qz, kb, vb, lambdas[lb], g_sub_b[lb], lam_init, groups)
            else:
                main = _attn_sample(qz, cache_k, cache_v, kb, vb, lambdas[lb], g_sub_b[lb], lam_init)
            w_main, lm = W["w_out_b"], lb
        mem = _memattn(mq, mem_k, mem_v, l)
        x = _mix_ffn(main, mem, x, w_main, lm, W["w_out_mem"], g[l, 3], g[l, 4], W["w_up"], W["w_down"], g[l, 5], l)
        if l == N_A_LAYERS - 1:
            if prompt:
                k_new, v_new, kb, vb = _kvproj_seq(x, g_kv, W["w_kv"], W["w_kt"], groups)
            else:
                k_new, v_new, kb, vb = _kvproj(x, g_kv, W["w_kv"])
    return x, states, k_new, v_new


def _unpack_states(states, lead):
    hd = MLSTM_HEAD_DIM
    ct = jnp.stack([s[0] for s in states])
    m = jnp.stack([s[1] for s in states])
    c = jnp.swapaxes(ct[..., :hd, :hd], -1, -2)
    n = ct[..., :hd, N_COL]
    return c, n, m[..., :MLSTM_HEADS, 0]


def kernel(x_prompt, x_sample, mem_prompt, cache_mem_k, cache_mem_v, state_C, state_n, state_m, cache_k, cache_v,
           w_up, w_down, norm_g, w_in_a, b_gates, g_head_a, w_in_b, lambdas, g_sub_b, g_kv, w_kv, w_mem_kv, w_out):
    W = _prep_weights(w_up, w_down, norm_g, w_in_a, b_gates, g_head_a, w_in_b, w_kv, w_mem_kv, w_out)
    g_kv2 = g_kv.reshape(1, D_MODEL)
    g_sub = g_sub_b.reshape(-1, 1, DIFF_V_DIM)
    hd = MLSTM_HEAD_DIM

    bp, tp, _ = x_prompt.shape
    mkt_p, mvt_p = _memkv(mem_prompt, W["w_mem_kv_t"])
    ct0 = jnp.zeros((N_A_LAYERS, bp, MLSTM_HEADS, HEAD_PAD, HEAD_PAD), F32)
    m0 = jnp.zeros((N_A_LAYERS, bp, 8, LANES), F32)
    y_p, st_p, kt_p, v4_p = _trunk(x_prompt.reshape(bp * tp, D_MODEL), bp, mkt_p, mvt_p, ct0, m0, None, None,
                                   W, lambdas, g_sub, g_kv2)
    c_p, n_p, m_p = _unpack_states(st_p, bp)

    bs, ts, _ = x_sample.shape
    pad = HEAD_PAD - hd
    ct_s = jnp.concatenate([jnp.swapaxes(state_C, -1, -2), state_n[..., None],
                            jnp.zeros(state_n.shape + (pad - 1,), F32)], axis=-1)
    ct_s = jnp.pad(ct_s, [(0, 0)] * 3 + [(0, pad), (0, 0)])
    m_s = jnp.broadcast_to(jnp.pad(state_m, ((0, 0), (0, 0), (0, 8 - MLSTM_HEADS)))[..., None],
                           (N_A_LAYERS, bs, 8, LANES))
    mem_t = lambda a: jnp.transpose(a, (0, 1, 3, 4, 2)).reshape(DEPTH, -1, MEM_WIDTH, N_MEM)
    y_s, st_s, k_s, v_s = _trunk(
        x_sample.reshape(bs * ts, D_MODEL), 1, mem_t(cache_mem_k), mem_t(cache_mem_v), ct_s, m_s,
        jnp.transpose(cache_k, (0, 2, 3, 1)).reshape(bs, MAIN_WIDTH, -1), jnp.transpose(cache_v, (0, 2, 1, 3)),
        W, lambdas, g_sub, g_kv2)
    c_s, n_s, m_sn = _unpack_states(st_s, bs)

    kshape = (2 * DIFF_HEADS, DIFF_HEAD_DIM)
    vshape = (DIFF_HEADS, DIFF_V_DIM)
    mem_out = lambda a: jnp.transpose(a.reshape(DEPTH, bp, MEM_HEADS, MEM_HEAD_DIM, N_MEM), (0, 1, 4, 2, 3))
    return (y_p.reshape(bp, tp, D_MODEL), y_s.reshape(bs, ts, D_MODEL),
            mem_out(mkt_p), mem_out(mvt_p),
            c_p, n_p, m_p,
            jnp.transpose(kt_p.reshape((bp,) + kshape + (tp,)), (0, 3, 1, 2)), jnp.transpose(v4_p, (0, 2, 1, 3)),
            c_s, n_s, m_sn,
            k_s.reshape((bs, ts) + kshape), v_s.reshape((bs, ts) + vshape))
```

```python
import functools
import math

import jax
import jax.numpy as jnp
from jax import lax
from jax.experimental import pallas as pl
from jax.experimental.pallas import tpu as pltpu

F32 = jnp.float32
BF16 = jnp.bfloat16

D_MODEL = 1024
DEPTH = 4
CHUNK = 64
N_MEM = 256
N_A_LAYERS = DEPTH // 2
MEM_HEADS = 4
MEM_WIDTH = 256
MEM_HEAD_DIM = 64
MAIN_WIDTH = 768
MLSTM_HEADS = 4
MLSTM_HEAD_DIM = 192
DIFF_HEAD_DIM = 64
DIFF_V_DIM = 128
DIFF_HEADS = 6
FFN_DIM = 2816
EPS = 1e-6

LANES = 128
HEAD_PAD = 256
N_COL = MLSTM_HEAD_DIM
A_WIDTH = MLSTM_HEADS * HEAD_PAD
QZ_WIDTH = 2 * DIFF_HEADS * LANES
MQZ_WIDTH = MEM_HEADS * LANES
LOG2E = math.log2(math.e)
VMEM_LIMIT = 56 * 1024 * 1024

TOKEN_TILE = 512
FFN_ROW_SPLIT = 2
MLSTM_TILE = 512
MLSTM_SEQ_CHUNK = 256
MLSTM_GROUPS_PER_STEP = 4
ATTN_TQ = 512
MEM_TILE = 512


def _params(*sem):
    return pltpu.CompilerParams(dimension_semantics=sem, vmem_limit_bytes=VMEM_LIMIT)


def _rms(x, g):
    return x * lax.rsqrt(jnp.mean(x * x, axis=-1, keepdims=True) + EPS) * g


def _log_sigmoid(x):
    return jnp.minimum(x, 0.0) - jnp.log1p(jnp.exp(-jnp.abs(x)))


def _dot(a, b):
    return jnp.dot(a, b, preferred_element_type=F32)


def _dot_nt(a, b):
    return lax.dot_general(a, b, (((1,), (1,)), ((), ())), preferred_element_type=F32)


def _resident(shape, index_map):
    return pl.BlockSpec(shape, index_map, pipeline_mode=pl.Buffered(1))


def _ffn_rows(x, g0_ref, wg_ref, wu_ref, wd_ref, g1_ref):
    xb = _rms(x, g0_ref[...]).astype(BF16)
    hg = _dot(xb, wg_ref[...])
    hu = _dot(xb, wu_ref[...])
    a = (hg * jax.nn.sigmoid(hg) * hu).astype(BF16)
    y = _dot(a, wd_ref[...])
    return x + 0.5 * _rms(y, g1_ref[...])


def _row_groups(tm):
    sub = tm // FFN_ROW_SPLIT
    return [slice(r * sub, (r + 1) * sub) for r in range(FFN_ROW_SPLIT)]


def _ffn_kernel(x_ref, g0_ref, wg_ref, wu_ref, wd_ref, g1_ref, o_ref):
    for rows in _row_groups(x_ref.shape[0]):
        o_ref[rows, :] = _ffn_rows(x_ref[rows, :], g0_ref, wg_ref, wu_ref, wd_ref, g1_ref)


def _mix_ffn_kernel(main_ref, mem_ref, x_ref, wm_ref, we_ref, g3_ref, g0_ref, wg_ref, wu_ref, wd_ref, g1_ref, o_ref):
    groups = _row_groups(x_ref.shape[0])
    xs = []
    for rows in groups:
        y = _dot(main_ref[rows, :], wm_ref[...]) + _dot(mem_ref[rows, :], we_ref[...])
        xs.append(x_ref[rows, :] + _rms(y, g3_ref[...]))
    for rows, x in zip(groups, xs):
        o_ref[rows, :] = _ffn_rows(x, g0_ref, wg_ref, wu_ref, wd_ref, g1_ref)


def _mix_ffn(main, mem, x, w_main, lm, w_mem, g3, g0, w_up, w_down, g1, l):
    n = x.shape[0]
    tm = min(TOKEN_TILE, n)
    wm = main.shape[1]
    row = lambda w: pl.BlockSpec((tm, w), lambda i: (i, 0))
    gain = pl.BlockSpec((1, D_MODEL), lambda i: (0, 0))
    return pl.pallas_call(
        _mix_ffn_kernel,
        out_shape=jax.ShapeDtypeStruct((n, D_MODEL), F32),
        grid=(n // tm,),
        in_specs=[
            row(wm), row(MEM_WIDTH), row(D_MODEL),
            _resident((None, wm, D_MODEL), lambda i: (lm, 0, 0)),
            _resident((None, MEM_WIDTH, D_MODEL), lambda i: (l, 0, 0)),
            gain, gain,
            _resident((None, None, D_MODEL, FFN_DIM), lambda i: (l, 1, 0, 0)),
            _resident((None, None, D_MODEL, FFN_DIM), lambda i: (l, 1, 0, 1)),
            _resident((None, None, FFN_DIM, D_MODEL), lambda i: (l, 1, 0, 0)),
            gain,
        ],
        out_specs=row(D_MODEL),
        compiler_params=_params("arbitrary"),
        name="mix_ffn",
    )(main, mem, x, w_main, w_mem, g3, g0, w_up, w_up, w_down, g1)


def _ffn(x, g0, w_up, w_down, g1, l, k):
    n = x.shape[0]
    tm = min(TOKEN_TILE, n)
    return pl.pallas_call(
        _ffn_kernel,
        out_shape=jax.ShapeDtypeStruct((n, D_MODEL), F32),
        grid=(n // tm,),
        in_specs=[
            pl.BlockSpec((tm, D_MODEL), lambda i: (i, 0)),
            pl.BlockSpec((1, D_MODEL), lambda i: (0, 0)),
            _resident((None, None, D_MODEL, FFN_DIM), lambda i: (l, k, 0, 0)),
            _resident((None, None, D_MODEL, FFN_DIM), lambda i: (l, k, 0, 1)),
            _resident((None, None, FFN_DIM, D_MODEL), lambda i: (l, k, 0, 0)),
            pl.BlockSpec((1, D_MODEL), lambda i: (0, 0)),
        ],
        out_specs=pl.BlockSpec((tm, D_MODEL), lambda i: (i, 0)),
        compiler_params=_params("arbitrary"),
        name="ffn",
    )(x, g0, w_up, w_up, w_down, g1)


def _inproj_a_kernel(x_ref, g_ref, wrow_ref, wkt_ref, wgt_ref, brow_ref, bcol_ref,
                     q_ref, kt_ref, v_ref, o_ref, mq_ref, gcol_ref, grow_ref):
    xb = _rms(x_ref[...], g_ref[...]).astype(BF16)
    y = _dot(xb, wrow_ref[...])
    q_ref[...] = y[:, 0:A_WIDTH].astype(BF16)
    v = y[:, A_WIDTH:2 * A_WIDTH]
    lane = lax.broadcasted_iota(jnp.int32, v.shape, 1)
    v_ref[...] = jnp.where(lane % HEAD_PAD == N_COL, 1.0, v).astype(BF16)
    o_ref[...] = y[:, 2 * A_WIDTH:3 * A_WIDTH]
    mq_ref[...] = (y[:, 3 * A_WIDTH:3 * A_WIDTH + MQZ_WIDTH] * (MEM_HEAD_DIM ** -0.5 * LOG2E)).astype(BF16)
    g = y[:, 3 * A_WIDTH + MQZ_WIDTH:] + brow_ref[...]
    glane = lax.broadcasted_iota(jnp.int32, g.shape, 1)
    gcol_ref[...] = jnp.where(glane < MLSTM_HEADS, g, _log_sigmoid(g))
    kt = _dot_nt(wkt_ref[...], xb)
    kt_ref[...] = (kt * (MLSTM_HEAD_DIM ** -0.5)).astype(BF16)
    gt = _dot_nt(wgt_ref[...], xb) + bcol_ref[...]
    grow_idx = lax.broadcasted_iota(jnp.int32, gt.shape, 0)
    grow_ref[...] = jnp.where(grow_idx < MLSTM_HEADS, gt, _log_sigmoid(gt))


def _inproj_a(x, g, wrow, wkt, wgt, brow, bcol, l, groups):
    n = x.shape[0]
    tg = n // groups
    tm = min(TOKEN_TILE, tg)
    per = tg // tm
    wcols = wrow.shape[2]
    outs = (
        jax.ShapeDtypeStruct((n, A_WIDTH), BF16),
        jax.ShapeDtypeStruct((groups, A_WIDTH, tg), BF16),
        jax.ShapeDtypeStruct((n, A_WIDTH), BF16),
        jax.ShapeDtypeStruct((n, A_WIDTH), F32),
        jax.ShapeDtypeStruct((n, MQZ_WIDTH), BF16),
        jax.ShapeDtypeStruct((n, LANES), F32),
        jax.ShapeDtypeStruct((groups, 8, tg), F32),
    )
    row = lambda w: pl.BlockSpec((tm, w), lambda i: (i, 0))
    return pl.pallas_call(
        _inproj_a_kernel,
        out_shape=outs,
        grid=(n // tm,),
        in_specs=[
            row(D_MODEL),
            pl.BlockSpec((1, D_MODEL), lambda i: (0, 0)),
            _resident((None, D_MODEL, wcols), lambda i: (l, 0, 0)),
            _resident((None, A_WIDTH, D_MODEL), lambda i: (l, 0, 0)),
            _resident((None, 8, D_MODEL), lambda i: (l, 0, 0)),
            pl.BlockSpec((None, 1, LANES), lambda i: (l, 0, 0)),
            pl.BlockSpec((None, 8, 1), lambda i: (l, 0, 0)),
        ],
        out_specs=(
            row(A_WIDTH),
            pl.BlockSpec((None, A_WIDTH, tm), lambda i: (i // per, 0, i % per)),
            row(A_WIDTH),
            row(A_WIDTH),
            row(MQZ_WIDTH),
            row(LANES),
            pl.BlockSpec((None, 8, tm), lambda i: (i // per, 0, i % per)),
        ),
        compiler_params=_params("arbitrary"),
        name="inproj_a",
    )(x, g, wrow, wkt, wgt, brow, bcol)


def _inproj_b_kernel(x_ref, g_ref, w_ref, qz_ref, mq_ref):
    xb = _rms(x_ref[...], g_ref[...]).astype(BF16)
    y = _dot(xb, w_ref[...])
    qz_ref[...] = (y[:, :QZ_WIDTH] * (DIFF_HEAD_DIM ** -0.5 * LOG2E)).astype(BF16)
    mq_ref[...] = (y[:, QZ_WIDTH:] * (MEM_HEAD_DIM ** -0.5 * LOG2E)).astype(BF16)


def _inproj_b(x, g, w, lb):
    n = x.shape[0]
    tm = min(TOKEN_TILE, n)
    return pl.pallas_call(
        _inproj_b_kernel,
        out_shape=(jax.ShapeDtypeStruct((n, QZ_WIDTH), BF16), jax.ShapeDtypeStruct((n, MQZ_WIDTH), BF16)),
        grid=(n // tm,),
        in_specs=[
            pl.BlockSpec((tm, D_MODEL), lambda i: (i, 0)),
            pl.BlockSpec((1, D_MODEL), lambda i: (0, 0)),
            _resident((None, D_MODEL, QZ_WIDTH + MQZ_WIDTH), lambda i: (lb, 0, 0)),
        ],
        out_specs=(pl.BlockSpec((tm, QZ_WIDTH), lambda i: (i, 0)), pl.BlockSpec((tm, MQZ_WIDTH), lambda i: (i, 0))),
        compiler_params=_params("arbitrary"),
        name="inproj_b",
    )(x, g, w)


def _kvproj_kernel(x_ref, g_ref, w_ref, k_ref, v_ref, kb_ref, vb_ref):
    xb = _rms(x_ref[...], g_ref[...]).astype(BF16)
    y = _dot(xb, w_ref[...])
    k = y[:, :MAIN_WIDTH]
    v = y[:, MAIN_WIDTH:]
    k_ref[...] = k
    v_ref[...] = v
    kb_ref[...] = k.astype(BF16)
    vb_ref[...] = v.astype(BF16)


def _kvproj(x, g, w):
    n = x.shape[0]
    tm = min(TOKEN_TILE, n)
    blk = pl.BlockSpec((tm, MAIN_WIDTH), lambda i: (i, 0))
    return pl.pallas_call(
        _kvproj_kernel,
        out_shape=(jax.ShapeDtypeStruct((n, MAIN_WIDTH), F32), jax.ShapeDtypeStruct((n, MAIN_WIDTH), F32),
                   jax.ShapeDtypeStruct((n, MAIN_WIDTH), BF16), jax.ShapeDtypeStruct((n, MAIN_WIDTH), BF16)),
        grid=(n // tm,),
        in_specs=[
            pl.BlockSpec((tm, D_MODEL), lambda i: (i, 0)),
            pl.BlockSpec((1, D_MODEL), lambda i: (0, 0)),
            _resident((D_MODEL, 2 * MAIN_WIDTH), lambda i: (0, 0)),
        ],
        out_specs=(blk, blk, blk, blk),
        compiler_params=_params("arbitrary"),
        name="kvproj",
    )(x, g, w)


def _kvproj_seq_kernel(x_ref, g_ref, w_ref, wkt_ref, kt_ref, v4_ref, kb_ref, vb_ref):
    xb = _rms(x_ref[...], g_ref[...]).astype(BF16)
    y = _dot(xb, w_ref[...])
    v = y[:, MAIN_WIDTH:]
    kb_ref[...] = y[:, :MAIN_WIDTH].astype(BF16)
    vb_ref[...] = v.astype(BF16)
    kt_ref[...] = _dot_nt(wkt_ref[...], xb)
    for h in range(DIFF_HEADS):
        v4_ref[h] = v[:, h * DIFF_V_DIM:(h + 1) * DIFF_V_DIM]


def _kvproj_seq(x, g, w, wkt, groups):
    n = x.shape[0]
    tg = n // groups
    tm = min(TOKEN_TILE, tg)
    per = tg // tm
    blk = pl.BlockSpec((tm, MAIN_WIDTH), lambda i: (i, 0))
    return pl.pallas_call(
        _kvproj_seq_kernel,
        out_shape=(jax.ShapeDtypeStruct((groups, MAIN_WIDTH, tg), F32),
                   jax.ShapeDtypeStruct((groups, DIFF_HEADS, tg, DIFF_V_DIM), F32),
                   jax.ShapeDtypeStruct((n, MAIN_WIDTH), BF16), jax.ShapeDtypeStruct((n, MAIN_WIDTH), BF16)),
        grid=(n // tm,),
        in_specs=[
            pl.BlockSpec((tm, D_MODEL), lambda i: (i, 0)),
            pl.BlockSpec((1, D_MODEL), lambda i: (0, 0)),
            _resident((D_MODEL, 2 * MAIN_WIDTH), lambda i: (0, 0)),
            _resident((MAIN_WIDTH, D_MODEL), lambda i: (0, 0)),
        ],
        out_specs=(pl.BlockSpec((None, MAIN_WIDTH, tm), lambda i: (i // per, 0, i % per)),
                   pl.BlockSpec((None, DIFF_HEADS, tm, DIFF_V_DIM), lambda i: (i // per, 0, i % per, 0)),
                   blk, blk),
        compiler_params=_params("arbitrary"),
        name="kvproj_seq",
    )(x, g, w, wkt)


def _memkv_kernel(m_ref, wt_ref, kt_ref, vt_ref):
    yt = _dot_nt(wt_ref[...], m_ref[...].astype(BF16))
    kt_ref[...] = yt[:MEM_WIDTH]
    vt_ref[...] = yt[MEM_WIDTH:]


def _memkv(mem, wt):
    b = mem.shape[0]
    out = jax.ShapeDtypeStruct((DEPTH, b, MEM_WIDTH, N_MEM), F32)
    blk = pl.BlockSpec((None, None, MEM_WIDTH, N_MEM), lambda l, i: (l, i, 0, 0))
    return pl.pallas_call(
        _memkv_kernel,
        out_shape=(out, out),
        grid=(DEPTH, b),
        in_specs=[pl.BlockSpec((None, N_MEM, D_MODEL), lambda l, i: (i, 0, 0)),
                  pl.BlockSpec((None, 2 * MEM_WIDTH, D_MODEL), lambda l, i: (l, 0, 0))],
        out_specs=(blk, blk),
        compiler_params=_params("arbitrary", "arbitrary"),
        name="memkv",
    )(mem, wt)


def _split3(x):
    hi = x.astype(BF16)
    r = x - hi.astype(F32)
    mid = r.astype(BF16)
    lo = (r - mid.astype(F32)).astype(BF16)
    return hi, mid, lo


def _mlstm_kernel(q_ref, kt_ref, v_ref, o_ref, gcol_ref, grow_ref, gh_ref, ct0_ref, m0_ref,
                  h_ref, ctn_ref, mn_ref, ct_s, m_s, tri_s, *, sequential, chunk):
    t = pl.program_id(1)
    nb, tb = q_ref.shape[0], q_ref.shape[1]
    win = max(chunk, LANES)
    units = [(b, c * chunk) for b in range(nb) for c in range(tb // chunk)]

    @pl.when((pl.program_id(0) == 0) & (t == 0))
    def _init_tri():
        r = lax.broadcasted_iota(jnp.int32, (tb, tb), 0)
        c = lax.broadcasted_iota(jnp.int32, (tb, tb), 1)
        tri_s[...] = jnp.where((r // chunk == c // chunk) & (c <= r), 1.0, 0.0).astype(BF16)

    @pl.when((t == 0) | (not sequential))
    def _init_state():
        ct_s[...] = ct0_ref[...]
        m_s[...] = m0_ref[...]

    tri = tri_s[...]
    bcols = [sum(_dot(tri, part) for part in _split3(gcol_ref[b])) for b in range(nb)]
    brows = [sum(_dot_nt(part, tri) for part in _split3(grow_ref[b])) for b in range(nb)]

    row_i = lax.broadcasted_iota(jnp.int32, (chunk, win), 0)
    lane_i = lax.broadcasted_iota(jnp.int32, (chunk, win), 1)
    lane_row = lax.broadcasted_iota(jnp.int32, (1, win), 1)
    pad_lane = lax.broadcasted_iota(jnp.int32, (chunk, HEAD_PAD), 1)

    pairs = []
    for slot, (b, r0) in enumerate(units):
        p0 = r0 // win * win
        off = r0 - p0
        in_chunk = (lane_i >= off) & (lane_i < off + chunk)
        vis = in_chunk & (lane_i - off <= row_i)
        in_chunk_row = (lane_row >= off) & (lane_row < off + chunk)
        for h in range(MLSTM_HEADS):
            pairs.append(dict(slot=slot, b=b, r0=r0, p0=p0, off=off, h=h, vis=vis, in_chunk_row=in_chunk_row,
                              hs=slice(h * HEAD_PAD, (h + 1) * HEAD_PAD)))

    for p in pairs:
        fl = MLSTM_HEADS + p["h"]
        bcol = bcols[p["b"]][p["r0"]:p["r0"] + chunk, fl:fl + 1]
        p["brow"] = brows[p["b"]][fl:fl + 1, p["p0"]:p["p0"] + win]
        p["igrow"] = grow_ref[p["b"], p["h"]:p["h"] + 1, p["p0"]:p["p0"] + win]
        p["m_prev"] = m_s[p["slot"], p["h"]:p["h"] + 1, 0:1]
        dlog = jnp.where(p["vis"], bcol - p["brow"] + p["igrow"], -jnp.inf)
        a = jnp.max(dlog, axis=1, keepdims=True)
        p["mt"] = jnp.maximum(bcol + p["m_prev"], a)
        p["e"] = jnp.exp(dlog - p["mt"])
        p["w_inter"] = jnp.exp(bcol + p["m_prev"] - p["mt"])

    for p in pairs:
        b, r0, p0, hs = p["b"], p["r0"], p["p0"], p["hs"]
        qh = q_ref[b, r0:r0 + chunk, hs]
        p["kth"] = kt_ref[b, hs, p0:p0 + win]
        p["vh"] = v_ref[b, p0:p0 + win, hs]
        s = (_dot(qh, p["kth"]) * p["e"]).astype(BF16)
        p["ct"] = ct_s[p["slot"], p["h"]]
        p["num"] = p["w_inter"] * _dot(qh, p["ct"].astype(BF16)) + _dot(s, p["vh"])

    for p in pairs:
        b, r0, hs, num = p["b"], p["r0"], p["hs"], p["num"]
        nq = num[:, N_COL:N_COL + 1]
        den = jnp.maximum(jnp.abs(nq), jnp.exp(-p["mt"]))
        hh = jnp.where(pad_lane < MLSTM_HEAD_DIM, num / den, 0.0)
        ms = jnp.sum(hh * hh, axis=1, keepdims=True) * (1.0 / MLSTM_HEAD_DIM)
        hn = hh * lax.rsqrt(ms + EPS) * gh_ref[:, hs]
        h_ref[b, r0:r0 + chunk, hs] = (hn * jax.nn.sigmoid(o_ref[b, r0:r0 + chunk, hs])).astype(BF16)

    for p in pairs:
        last = p["off"] + chunk - 1
        b_last = p["brow"][:, last:last + 1]
        m_new = p["mt"][chunk - 1:chunk, :]
        w_end = jnp.where(p["in_chunk_row"], jnp.exp(b_last - p["brow"] + p["igrow"] - m_new), 0.0)
        decay = jnp.exp(b_last + p["m_prev"] - m_new)
        kw = (p["kth"].astype(F32) * w_end).astype(BF16)
        ct_s[p["slot"], p["h"]] = decay * p["ct"] + _dot(kw, p["vh"])
        m_s[p["slot"], p["h"]:p["h"] + 1, :] = jnp.broadcast_to(m_new, (1, LANES))

    @pl.when((t == pl.num_programs(1) - 1) | (not sequential))
    def _fin():
        ctn_ref[...] = ct_s[...]
        mn_ref[...] = m_s[...]


def _mlstm(q, kt, v, o, gcol, grow, ghead, ct0, m0, l, groups, sequential):
    n = q.shape[0]
    tg = n // groups
    if sequential:
        chunk = tb = min(MLSTM_SEQ_CHUNK, tg)
        nb = min(MLSTM_GROUPS_PER_STEP, groups)
    else:
        chunk, tb, nb = CHUNK, min(MLSTM_TILE, tg), 1
    steps = tg // tb
    slots = nb * (tb // chunk)
    streams = ct0.shape[1]
    sblk = (lambda g, t: g) if sequential else (lambda g, t: g * steps + t)
    rows3 = lambda a: a.reshape(groups, tg, a.shape[-1])
    row = lambda w: pl.BlockSpec((nb, tb, w), lambda g, t: (g, t, 0))
    col = lambda r: pl.BlockSpec((nb, r, tb), lambda g, t: (g, 0, t))
    kernel = functools.partial(_mlstm_kernel, sequential=sequential, chunk=chunk)
    h, ctn, mn = pl.pallas_call(
        kernel,
        out_shape=(jax.ShapeDtypeStruct((groups, tg, A_WIDTH), BF16),
                   jax.ShapeDtypeStruct((streams, MLSTM_HEADS, HEAD_PAD, HEAD_PAD), F32),
                   jax.ShapeDtypeStruct((streams, 8, LANES), F32)),
        grid=(groups // nb, steps),
        in_specs=[
            row(A_WIDTH),
            col(A_WIDTH),
            row(A_WIDTH),
            row(A_WIDTH),
            row(LANES),
            col(8),
            pl.BlockSpec((None, 1, A_WIDTH), lambda g, t: (l, 0, 0)),
            pl.BlockSpec((None, slots, MLSTM_HEADS, HEAD_PAD, HEAD_PAD), lambda g, t: (l, sblk(g, t), 0, 0, 0)),
            pl.BlockSpec((None, slots, 8, LANES), lambda g, t: (l, sblk(g, t), 0, 0)),
        ],
        out_specs=(
            row(A_WIDTH),
            pl.BlockSpec((slots, MLSTM_HEADS, HEAD_PAD, HEAD_PAD), lambda g, t: (sblk(g, t), 0, 0, 0)),
            pl.BlockSpec((slots, 8, LANES), lambda g, t: (sblk(g, t), 0, 0)),
        ),
        scratch_shapes=[
            pltpu.VMEM((slots, MLSTM_HEADS, HEAD_PAD, HEAD_PAD), F32),
            pltpu.VMEM((slots, 8, LANES), F32),
            pltpu.VMEM((tb, tb), BF16),
        ],
        compiler_params=_params("arbitrary", "arbitrary"),
        name="mlstm",
    )(rows3(q), kt, rows3(v), rows3(o), rows3(gcol), grow, ghead, ct0, m0)
    return h.reshape(n, A_WIDTH), ctn, mn


_SLOPES2 = tuple(LOG2E * 2.0 ** (-8.0 * (h + 1) / DIFF_HEADS) for h in range(DIFF_HEADS))


def _lam(lam_ref, lam_init):
    lp = lam_ref[...]
    s1 = jnp.sum(lp[0:1] * lp[1:2], axis=1, keepdims=True)
    s2 = jnp.sum(lp[2:3] * lp[3:4], axis=1, keepdims=True)
    return jnp.exp(s1) - jnp.exp(s2) + lam_init


def _vext(v_tile):
    return jnp.concatenate([v_tile, jnp.ones_like(v_tile)], axis=1)


def _finish_head(acc1, acc2, lam, gsub, lam_init):
    o = acc1[:, :DIFF_V_DIM] / acc1[:, DIFF_V_DIM:] - lam * (acc2[:, :DIFF_V_DIM] / acc2[:, DIFF_V_DIM:])
    return (_rms(o, gsub) * (1.0 - lam_init)).astype(BF16)


def _attn_prompt_kernel(it_ref, jt_ref, qz_ref, k_ref, v_ref, lam_ref, gsub_ref, o_ref, acc_s, m_s, *, lam_init, tq):
    step_id = pl.program_id(1)
    i = it_ref[step_id]
    j = jt_ref[step_id]
    tk = 2 * tq
    last = j == (i >> 1)
    odd = (i & 1) == 1

    @pl.when(j == 0)
    def _init():
        acc_s[...] = jnp.zeros_like(acc_s)
        m_s[...] = jnp.full_like(m_s, -jnp.inf)

    def step(keys, bias_fn):
        def scores(idx):
            h = idx % DIFF_HEADS
            qt = qz_ref[:, idx * LANES:(idx + 1) * LANES]
            kt_i = idx // DIFF_HEADS * (DIFF_HEADS // 2) + h // 2
            kt = k_ref[0:keys, kt_i * LANES:(kt_i + 1) * LANES]
            return bias_fn(_dot_nt(qt, kt), h)

        s_next = scores(0)
        for idx in range(2 * DIFF_HEADS):
            s = s_next
            if idx + 1 < 2 * DIFF_HEADS:
                s_next = scores(idx + 1)
            h = idx % DIFF_HEADS
            m_prev = m_s[idx]
            m_new = jnp.maximum(m_prev, jnp.max(s, axis=1, keepdims=True))
            p = jnp.exp2(s - jnp.concatenate([m_new] * (keys // LANES), axis=1)).astype(BF16)
            alpha = jnp.exp2(m_prev - m_new)
            vt = _vext(v_ref[0:keys, h * LANES:(h + 1) * LANES])
            acc_s[idx] = jnp.concatenate([alpha, alpha], axis=1) * acc_s[idx] + _dot(p, vt)
            m_s[idx] = m_new

    def finish():
        lam = _lam(lam_ref, lam_init)
        for h in range(DIFF_HEADS):
            o_ref[:, h * LANES:(h + 1) * LANES] = _finish_head(
                acc_s[h], acc_s[DIFF_HEADS + h], lam, gsub_ref[...], lam_init)

    def own_keys(keys):
        r = lax.broadcasted_iota(jnp.int32, (tq, keys), 0)
        c = lax.broadcasted_iota(jnp.int32, (tq, keys), 1) - (keys - tq)
        vis = (c < 0) | (jnp.maximum(c, 0) // CHUNK <= r // CHUNK)
        g = (r - jnp.abs(r - c)).astype(F32)
        step(keys, lambda s, h: jnp.where(vis, s + _SLOPES2[h] * g, -jnp.inf))
        finish()

    @pl.when(jnp.logical_not(last))
    def _past():
        col = lax.broadcasted_iota(jnp.int32, (1, tk), 1).astype(F32)
        off = (j * tk - i * tq).astype(F32)
        step(tk, lambda s, h: s + _SLOPES2[h] * (col + off))

    @pl.when(last & odd)
    def _diag_odd():
        own_keys(tk)

    @pl.when(last & jnp.logical_not(odd))
    def _diag_even():
        own_keys(tq)


def _attn_prompt(qz, kb, vb, lambdas, gsub, lam_init, groups):
    n = qz.shape[0]
    tg = n // groups
    tq = min(ATTN_TQ, tg // 2)
    tk = 2 * tq
    nq = tg // tq
    nkv = tg // tk
    steps = [(i, j) for i in range(nq) for j in range(i // 2 + 1)]
    itab = jnp.asarray([s[0] for s in steps], jnp.int32)
    jtab = jnp.asarray([s[1] for s in steps], jnp.int32)
    kernel = functools.partial(_attn_prompt_kernel, lam_init=lam_init, tq=tq)
    q_spec = lambda w: pl.BlockSpec((tq, w), lambda b, s, it, jt: (b * nq + it[s], 0))
    kv_spec = pl.BlockSpec((tk, MAIN_WIDTH), lambda b, s, it, jt: (b * nkv + jt[s], 0))
    grid_spec = pltpu.PrefetchScalarGridSpec(
        num_scalar_prefetch=2,
        grid=(groups, len(steps)),
        in_specs=[
            q_spec(QZ_WIDTH),
            kv_spec,
            kv_spec,
            pl.BlockSpec((4, DIFF_HEAD_DIM), lambda b, s, it, jt: (0, 0)),
            pl.BlockSpec((1, DIFF_V_DIM), lambda b, s, it, jt: (0, 0)),
        ],
        out_specs=q_spec(MAIN_WIDTH),
        scratch_shapes=[
            pltpu.VMEM((2 * DIFF_HEADS, tq, 2 * DIFF_V_DIM), F32),
            pltpu.VMEM((2 * DIFF_HEADS, tq, LANES), F32),
        ],
    )
    return pl.pallas_call(
        kernel,
        out_shape=jax.ShapeDtypeStruct((n, MAIN_WIDTH), BF16),
        grid_spec=grid_spec,
        compiler_params=_params("arbitrary", "arbitrary"),
        name="attn_prompt",
    )(itab, jtab, qz, kb, vb, lambdas, gsub)


def _attn_sample_kernel(qz_ref, ckt_ref, cv_ref, nk_ref, nv_ref, lam_ref, gsub_ref, o_ref, *, lam_init, past):
    tq = qz_ref.shape[0]

    def geometry(width, c0):
        r = lax.broadcasted_iota(jnp.int32, (tq, width), 0) + past
        c = lax.broadcasted_iota(jnp.int32, (tq, width), 1) + c0
        return c // CHUNK <= r // CHUNK, jnp.abs(r - c).astype(F32)

    vis_c, dist_c = geometry(past, 0)
    vis_n, dist_n = geometry(tq, past)
    lam = _lam(lam_ref, lam_init)
    probs = {}
    for kt_i in range(DIFF_HEADS):
        a, h0 = kt_i // (DIFF_HEADS // 2), 2 * (kt_i % (DIFF_HEADS // 2))
        ks = slice(kt_i * LANES, (kt_i + 1) * LANES)
        q2 = jnp.concatenate([qz_ref[:, (a * DIFF_HEADS + h) * LANES:(a * DIFF_HEADS + h + 1) * LANES]
                              for h in (h0, h0 + 1)], axis=0)
        sc2 = _dot(q2, ckt_ref[ks, :].astype(BF16))
        sn2 = _dot_nt(q2, nk_ref[:, ks])
        for r, h in enumerate((h0, h0 + 1)):
            rows = slice(r * tq, (r + 1) * tq)
            s_c = jnp.where(vis_c, sc2[rows] - _SLOPES2[h] * dist_c, -jnp.inf)
            s_n = jnp.where(vis_n, sn2[rows] - _SLOPES2[h] * dist_n, -jnp.inf)
            m = jnp.maximum(jnp.max(s_c, axis=1, keepdims=True), jnp.max(s_n, axis=1, keepdims=True))
            probs[a, h] = (jnp.exp2(s_c - m).astype(BF16), jnp.exp2(s_n - m).astype(BF16))
    for h in range(DIFF_HEADS):
        p_c = jnp.concatenate([probs[0, h][0], probs[1, h][0]], axis=0)
        p_n = jnp.concatenate([probs[0, h][1], probs[1, h][1]], axis=0)
        acc = (_dot(p_c, _vext(cv_ref[h].astype(BF16)))
               + _dot(p_n, _vext(nv_ref[:, h * LANES:(h + 1) * LANES])))
        o_ref[:, h * LANES:(h + 1) * LANES] = _finish_head(acc[0:tq], acc[tq:2 * tq], lam, gsub_ref[...], lam_init)


def _attn_sample(qz, cache_kt, cache_v, kb, vb, lambdas, gsub, lam_init):
    streams, _, past = cache_kt.shape
    n = qz.shape[0]
    tq = n // streams
    kernel = functools.partial(_attn_sample_kernel, lam_init=lam_init, past=past)
    new = pl.BlockSpec((tq, MAIN_WIDTH), lambda b: (b, 0))
    return pl.pallas_call(
        kernel,
        out_shape=jax.ShapeDtypeStruct((n, MAIN_WIDTH), BF16),
        grid=(streams,),
        in_specs=[
            pl.BlockSpec((tq, QZ_WIDTH), lambda b: (b, 0)),
            pl.BlockSpec((None, MAIN_WIDTH, past), lambda b: (b, 0, 0)),
            pl.BlockSpec((None, DIFF_HEADS, past, DIFF_V_DIM), lambda b: (b, 0, 0, 0)),
            new, new,
            pl.BlockSpec((4, DIFF_HEAD_DIM), lambda b: (0, 0)),
            pl.BlockSpec((1, DIFF_V_DIM), lambda b: (0, 0)),
        ],
        out_specs=new,
        compiler_params=_params("arbitrary"),
        name="attn_sample",
    )(qz, cache_kt, cache_v, kb, vb, lambdas, gsub)


def _memattn_kernel(mq_ref, mkt_ref, mvt_ref, o_ref, *, tq):
    lane = lax.broadcasted_iota(jnp.int32, (tq, LANES), 1)
    for i in range(mkt_ref.shape[0]):
        rows = slice(i * tq, (i + 1) * tq)
        for pair in range(MEM_HEADS // 2):
            ks = slice(pair * LANES, (pair + 1) * LANES)
            mkt = mkt_ref[i, ks, :].astype(BF16)
            mvt = mvt_ref[i, ks, :].astype(BF16)
            outs = []
            for h in (2 * pair, 2 * pair + 1):
                s = _dot(mq_ref[rows, h * LANES:(h + 1) * LANES], mkt)
                p = jnp.exp2(s - jnp.max(s, axis=1, keepdims=True))
                l = jnp.sum(p, axis=1, keepdims=True)
                outs.append(_dot_nt(p.astype(BF16), mvt) / l)
            o_ref[rows, ks] = jnp.where(lane < MEM_HEAD_DIM, outs[0], outs[1]).astype(BF16)


def _memattn(mq, mkt, mvt, l):
    n = mq.shape[0]
    streams = mkt.shape[1]
    tg = n // streams
    tq = min(MEM_TILE, tg)
    per = tg // tq
    ns = MEM_TILE // tq
    rows = ns * tq
    mem = pl.BlockSpec((None, ns, MEM_WIDTH, N_MEM), lambda i: (l, i // per, 0, 0))
    return pl.pallas_call(
        functools.partial(_memattn_kernel, tq=tq),
        out_shape=jax.ShapeDtypeStruct((n, MEM_WIDTH), BF16),
        grid=(n // rows,),
        in_specs=[pl.BlockSpec((rows, MQZ_WIDTH), lambda i: (i, 0)), mem, mem],
        out_specs=pl.BlockSpec((rows, MEM_WIDTH), lambda i: (i, 0)),
        compiler_params=_params("arbitrary"),
        name="memattn",
    )(mq, mkt, mvt)


def _pad_heads(w, heads, dim, pad):
    lead = w.shape[:-1]
    w = w.reshape(lead + (heads, dim))
    w = jnp.pad(w, [(0, 0)] * len(lead) + [(0, 0), (0, pad - dim)])
    return w.reshape(lead + (heads * pad,))


def _interleave64(w, heads):
    lead = w.shape[:-1]
    w = w.reshape(lead + (heads, 64))
    z = jnp.zeros_like(w)
    even = jnp.concatenate([w, z], axis=-1)
    odd = jnp.concatenate([z, w], axis=-1)
    sel = (jnp.arange(heads) % 2 == 0)[:, None]
    return jnp.where(sel, even, odd).reshape(lead + (heads * LANES,))


def _prep_weights(w_up, w_down, norm_g, w_in_a, b_gates, g_head_a, w_in_b, w_kv, w_mem_kv, w_out):
    mw = MAIN_WIDTH
    prep = {
        "w_up": w_up.astype(BF16),
        "w_down": w_down.astype(BF16),
        "norm_g": norm_g.reshape(DEPTH, 6, 1, D_MODEL),
        "w_kv": w_kv.astype(BF16),
        "w_kt": w_kv[:, :mw].T.astype(BF16),
        "w_mem_kv_t": jnp.swapaxes(w_mem_kv, 1, 2).astype(BF16),
        "w_out_mem": w_out[:, mw:, :].astype(BF16),
        "w_out_b": w_out[N_A_LAYERS:, :mw, :].astype(BF16),
    }
    wa = w_out[:N_A_LAYERS, :mw, :].reshape(N_A_LAYERS, MLSTM_HEADS, MLSTM_HEAD_DIM, D_MODEL)
    wa = jnp.pad(wa, ((0, 0), (0, 0), (0, HEAD_PAD - MLSTM_HEAD_DIM), (0, 0)))
    prep["w_out_a"] = wa.reshape(N_A_LAYERS, A_WIDTH, D_MODEL).astype(BF16)
    pad_a = lambda w: _pad_heads(w, MLSTM_HEADS, MLSTM_HEAD_DIM, HEAD_PAD)
    q, k, v, o = (pad_a(w_in_a[:, :, s * mw:(s + 1) * mw]) for s in range(4))
    gates = w_in_a[:, :, 4 * mw:4 * mw + 2 * MLSTM_HEADS]
    mq = _interleave64(w_in_a[:, :, 4 * mw + 2 * MLSTM_HEADS:], MEM_HEADS)
    gates_pad = jnp.pad(gates, ((0, 0), (0, 0), (0, LANES - 2 * MLSTM_HEADS)))
    prep["wrow_a"] = jnp.concatenate([q, v, o, mq, gates_pad], axis=-1).astype(BF16)
    prep["wkt_a"] = jnp.swapaxes(k, 1, 2).astype(BF16)
    prep["wgt_a"] = jnp.swapaxes(gates, 1, 2).astype(BF16)
    prep["brow_a"] = jnp.pad(b_gates, ((0, 0), (0, LANES - 2 * MLSTM_HEADS))).reshape(N_A_LAYERS, 1, LANES)
    prep["bcol_a"] = b_gates.reshape(N_A_LAYERS, 2 * MLSTM_HEADS, 1)
    prep["ghead_a"] = jnp.pad(g_head_a, ((0, 0), (0, 0), (0, HEAD_PAD - MLSTM_HEAD_DIM))).reshape(N_A_LAYERS, 1, A_WIDTH)
    nq = 2 * DIFF_HEADS * DIFF_HEAD_DIM
    prep["w_in_b"] = jnp.concatenate(
        [_interleave64(w_in_b[:, :, :nq], 2 * DIFF_HEADS), _interleave64(w_in_b[:, :, nq:], MEM_HEADS)],
        axis=-1).astype(BF16)
    return prep


def _trunk(x, groups, mem_k, mem_v, ct0, m0, cache_k, cache_v, W, lambdas, g_sub_b, g_kv):
    prompt = cache_k is None
    g = W["norm_g"]
    states = []
    k_new = v_new = kb = vb = None
    for l in range(DEPTH):
        x = _ffn(x, g[l, 0], W["w_up"], W["w_down"], g[l, 1], l, 0)
        if l < N_A_LAYERS:
            q, kt, v, o, mq, gcol, grow = _inproj_a(
                x, g[l, 2], W["wrow_a"], W["wkt_a"], W["wgt_a"], W["brow_a"], W["bcol_a"], l, groups)
            main, ctn, mn = _mlstm(q, kt, v, o, gcol, grow, W["ghead_a"], ct0, m0, l, groups, prompt)
            states.append((ctn, mn))
            w_main, lm = W["w_out_a"], l
        else:
            lb = l - N_A_LAYERS
            lam_init = 0.8 - 0.6 * math.exp(-0.3 * l)
            qz, mq = _inproj_b(x, g[l, 2], W["w_in_b"], lb)
            if prompt:
                main = _attn_prompt(qz, kb, vb, lambdas[lb], g_sub_b[lb], lam_init, groups)
            else:
                main = _attn_sample(qz, cache_k, cache_v, kb, vb, lambdas[lb], g_sub_b[lb], lam_init)
            w_main, lm = W["w_out_b"], lb
        mem = _memattn(mq, mem_k, mem_v, l)
        x = _mix_ffn(main, mem, x, w_main, lm, W["w_out_mem"], g[l, 3], g[l, 4], W["w_up"], W["w_down"], g[l, 5], l)
        if l == N_A_LAYERS - 1:
            if prompt:
                k_new, v_new, kb, vb = _kvproj_seq(x, g_kv, W["w_kv"], W["w_kt"], groups)
            else:
                k_new, v_new, kb, vb = _kvproj(x, g_kv, W["w_kv"])
    return x, states, k_new, v_new


def _unpack_states(states, lead):
    hd = MLSTM_HEAD_DIM
    ct = jnp.stack([s[0] for s in states])
    m = jnp.stack([s[1] for s in states])
    c = jnp.swapaxes(ct[..., :hd, :hd], -1, -2)
    n = ct[..., :hd, N_COL]
    return c, n, m[..., :MLSTM_HEADS, 0]


def kernel(x_prompt, x_sample, mem_prompt, cache_mem_k, cache_mem_v, state_C, state_n, state_m, cache_k, cache_v,
           w_up, w_down, norm_g, w_in_a, b_gates, g_head_a, w_in_b, lambdas, g_sub_b, g_kv, w_kv, w_mem_kv, w_out):
    W = _prep_weights(w_up, w_down, norm_g, w_in_a, b_gates, g_head_a, w_in_b, w_kv, w_mem_kv, w_out)
    g_kv2 = g_kv.reshape(1, D_MODEL)
    g_sub = g_sub_b.reshape(-1, 1, DIFF_V_DIM)
    hd = MLSTM_HEAD_DIM

    bp, tp, _ = x_prompt.shape
    mkt_p, mvt_p = _memkv(mem_prompt, W["w_mem_kv_t"])
    ct0 = jnp.zeros((N_A_LAYERS, bp, MLSTM_HEADS, HEAD_PAD, HEAD_PAD), F32)
    m0 = jnp.zeros((N_A_LAYERS, bp, 8, LANES), F32)
    y_p, st_p, kt_p, v4_p = _trunk(x_prompt.reshape(bp * tp, D_MODEL), bp, mkt_p, mvt_p, ct0, m0, None, None,
                                   W, lambdas, g_sub, g_kv2)
    c_p, n_p, m_p = _unpack_states(st_p, bp)

    bs, ts, _ = x_sample.shape
    pad = HEAD_PAD - hd
    ct_s = jnp.concatenate([jnp.swapaxes(state_C, -1, -2), state_n[..., None],
                            jnp.zeros(state_n.shape + (pad - 1,), F32)], axis=-1)
    ct_s = jnp.pad(ct_s, [(0, 0)] * 3 + [(0, pad), (0, 0)])
    m_s = jnp.broadcast_to(jnp.pad(state_m, ((0, 0), (0, 0), (0, 8 - MLSTM_HEADS)))[..., None],
                           (N_A_LAYERS, bs, 8, LANES))
    mem_t = lambda a: jnp.transpose(a, (0, 1, 3, 4, 2)).reshape(DEPTH, -1, MEM_WIDTH, N_MEM)
    y_s, st_s, k_s, v_s = _trunk(
        x_sample.reshape(bs * ts, D_MODEL), 1, mem_t(cache_mem_k), mem_t(cache_mem_v), ct_s, m_s,
        jnp.transpose(cache_k, (0, 2, 3, 1)).reshape(bs, MAIN_WIDTH, -1), jnp.transpose(cache_v, (0, 2, 1, 3)),
        W, lambdas, g_sub, g_kv2)
    c_s, n_s, m_sn = _unpack_states(st_s, bs)

    kshape = (2 * DIFF_HEADS, DIFF_HEAD_DIM)
    vshape = (DIFF_HEADS, DIFF_V_DIM)
    mem_out = lambda a: jnp.transpose(a.reshape(DEPTH, bp, MEM_HEADS, MEM_HEAD_DIM, N_MEM), (0, 1, 4, 2, 3))
    return (y_p.reshape(bp, tp, D_MODEL), y_s.reshape(bs, ts, D_MODEL),
            mem_out(mkt_p), mem_out(mvt_p),
            c_p, n_p, m_p,
            jnp.transpose(kt_p.reshape((bp,) + kshape + (tp,)), (0, 3, 1, 2)), jnp.transpose(v4_p, (0, 2, 1, 3)),
            c_s, n_s, m_sn,
            k_s.reshape((bs, ts) + kshape), v_s.reshape((bs, ts) + vshape))
```

```python
import functools
import math

import jax
import jax.numpy as jnp
from jax import lax
from jax.experimental import pallas as pl
from jax.experimental.pallas import tpu as pltpu

F32 = jnp.float32
BF16 = jnp.bfloat16

D_MODEL = 1024
DEPTH = 4
CHUNK = 64
N_MEM = 256
N_A_LAYERS = DEPTH // 2
MEM_HEADS = 4
MEM_WIDTH = 256
MEM_HEAD_DIM = 64
MAIN_WIDTH = 768
MLSTM_HEADS = 4
MLSTM_HEAD_DIM = 192
DIFF_HEAD_DIM = 64
DIFF_V_DIM = 128
DIFF_HEADS = 6
FFN_DIM = 2816
EPS = 1e-6

LANES = 128
HEAD_PAD = 256
N_COL = MLSTM_HEAD_DIM
A_WIDTH = MLSTM_HEADS * HEAD_PAD
QZ_WIDTH = 2 * DIFF_HEADS * LANES
MQZ_WIDTH = MEM_HEADS * LANES
LOG2E = math.log2(math.e)
VMEM_LIMIT = 56 * 1024 * 1024

TOKEN_TILE = 512
FFN_TILE = 1024
FFN_GROUP_ROWS = 256
MLSTM_TILE = 512
MLSTM_SEQ_CHUNK = 256
MLSTM_GROUPS_PER_STEP = 4
ATTN_TQ = 512
MEM_TILE = 512


def _params(*sem):
    return pltpu.CompilerParams(dimension_semantics=sem, vmem_limit_bytes=VMEM_LIMIT)


def _rms(x, g):
    return x * lax.rsqrt(jnp.mean(x * x, axis=-1, keepdims=True) + EPS) * g


def _log_sigmoid(x):
    return jnp.minimum(x, 0.0) - jnp.log1p(jnp.exp(-jnp.abs(x)))


def _dot(a, b):
    return jnp.dot(a, b, preferred_element_type=F32)


def _dot_nt(a, b):
    return lax.dot_general(a, b, (((1,), (1,)), ((), ())), preferred_element_type=F32)


def _resident(shape, index_map):
    return pl.BlockSpec(shape, index_map, pipeline_mode=pl.Buffered(1))


def _ffn_rows(x, g0_ref, wg_ref, wu_ref, wd_ref, g1_ref):
    xb = _rms(x, g0_ref[...]).astype(BF16)
    hg = _dot(xb, wg_ref[...])
    hu = _dot(xb, wu_ref[...])
    a = (hg * jax.nn.sigmoid(hg) * hu).astype(BF16)
    y = _dot(a, wd_ref[...])
    return x + 0.5 * _rms(y, g1_ref[...])


def _row_groups(tm):
    sub = min(FFN_GROUP_ROWS, tm)
    return [slice(r, r + sub) for r in range(0, tm, sub)]


def _ffn_kernel(x_ref, g0_ref, wg_ref, wu_ref, wd_ref, g1_ref, o_ref):
    for rows in _row_groups(x_ref.shape[0]):
        o_ref[rows, :] = _ffn_rows(x_ref[rows, :], g0_ref, wg_ref, wu_ref, wd_ref, g1_ref)


def _mix_ffn_kernel(main_ref, mem_ref, x_ref, wm_ref, we_ref, g3_ref, g0_ref, wg_ref, wu_ref, wd_ref, g1_ref, o_ref):
    groups = _row_groups(x_ref.shape[0])
    xs = []
    for rows in groups:
        y = _dot(main_ref[rows, :], wm_ref[...]) + _dot(mem_ref[rows, :], we_ref[...])
        xs.append(x_ref[rows, :] + _rms(y, g3_ref[...]))
    for rows, x in zip(groups, xs):
        o_ref[rows, :] = _ffn_rows(x, g0_ref, wg_ref, wu_ref, wd_ref, g1_ref)


def _mix_ffn(main, mem, x, w_main, lm, w_mem, g3, g0, w_up, w_down, g1, l):
    n = x.shape[0]
    tm = min(FFN_TILE, n)
    wm = main.shape[1]
    row = lambda w: pl.BlockSpec((tm, w), lambda i: (i, 0))
    gain = pl.BlockSpec((1, D_MODEL), lambda i: (0, 0))
    return pl.pallas_call(
        _mix_ffn_kernel,
        out_shape=jax.ShapeDtypeStruct((n, D_MODEL), F32),
        grid=(n // tm,),
        in_specs=[
            row(wm), row(MEM_WIDTH), row(D_MODEL),
            _resident((None, wm, D_MODEL), lambda i: (lm, 0, 0)),
            _resident((None, MEM_WIDTH, D_MODEL), lambda i: (l, 0, 0)),
            gain, gain,
            _resident((None, None, D_MODEL, FFN_DIM), lambda i: (l, 1, 0, 0)),
            _resident((None, None, D_MODEL, FFN_DIM), lambda i: (l, 1, 0, 1)),
            _resident((None, None, FFN_DIM, D_MODEL), lambda i: (l, 1, 0, 0)),
            gain,
        ],
        out_specs=row(D_MODEL),
        compiler_params=_params("arbitrary"),
        name="mix_ffn",
    )(main, mem, x, w_main, w_mem, g3, g0, w_up, w_up, w_down, g1)


def _ffn(x, g0, w_up, w_down, g1, l, k):
    n = x.shape[0]
    tm = min(FFN_TILE, n)
    return pl.pallas_call(
        _ffn_kernel,
        out_shape=jax.ShapeDtypeStruct((n, D_MODEL), F32),
        grid=(n // tm,),
        in_specs=[
            pl.BlockSpec((tm, D_MODEL), lambda i: (i, 0)),
            pl.BlockSpec((1, D_MODEL), lambda i: (0, 0)),
            _resident((None, None, D_MODEL, FFN_DIM), lambda i: (l, k, 0, 0)),
            _resident((None, None, D_MODEL, FFN_DIM), lambda i: (l, k, 0, 1)),
            _resident((None, None, FFN_DIM, D_MODEL), lambda i: (l, k, 0, 0)),
            pl.BlockSpec((1, D_MODEL), lambda i: (0, 0)),
        ],
        out_specs=pl.BlockSpec((tm, D_MODEL), lambda i: (i, 0)),
        compiler_params=_params("arbitrary"),
        name="ffn",
    )(x, g0, w_up, w_up, w_down, g1)


def _inproj_a_kernel(x_ref, g_ref, wrow_ref, wkt_ref, wgt_ref, brow_ref, bcol_ref,
                     q_ref, kt_ref, v_ref, o_ref, mq_ref, gcol_ref, grow_ref):
    for rows in _row_groups(x_ref.shape[0]):
        xb = _rms(x_ref[rows, :], g_ref[...]).astype(BF16)
        y = _dot(xb, wrow_ref[...])
        q_ref[rows, :] = y[:, 0:A_WIDTH].astype(BF16)
        v = y[:, A_WIDTH:2 * A_WIDTH]
        lane = lax.broadcasted_iota(jnp.int32, v.shape, 1)
        v_ref[rows, :] = jnp.where(lane % HEAD_PAD == N_COL, 1.0, v).astype(BF16)
        o_ref[rows, :] = y[:, 2 * A_WIDTH:3 * A_WIDTH]
        mq_ref[rows, :] = (y[:, 3 * A_WIDTH:3 * A_WIDTH + MQZ_WIDTH] * (MEM_HEAD_DIM ** -0.5 * LOG2E)).astype(BF16)
        g = y[:, 3 * A_WIDTH + MQZ_WIDTH:] + brow_ref[...]
        glane = lax.broadcasted_iota(jnp.int32, g.shape, 1)
        gcol_ref[rows, :] = jnp.where(glane < MLSTM_HEADS, g, _log_sigmoid(g))
        kt = _dot_nt(wkt_ref[...], xb)
        kt_ref[:, rows] = (kt * (MLSTM_HEAD_DIM ** -0.5)).astype(BF16)
        gt = _dot_nt(wgt_ref[...], xb) + bcol_ref[...]
        grow_idx = lax.broadcasted_iota(jnp.int32, gt.shape, 0)
        grow_ref[:, rows] = jnp.where(grow_idx < MLSTM_HEADS, gt, _log_sigmoid(gt))


def _inproj_a(x, g, wrow, wkt, wgt, brow, bcol, l, groups):
    n = x.shape[0]
    tg = n // groups
    tm = min(FFN_TILE, tg)
    per = tg // tm
    wcols = wrow.shape[2]
    outs = (
        jax.ShapeDtypeStruct((n, A_WIDTH), BF16),
        jax.ShapeDtypeStruct((groups, A_WIDTH, tg), BF16),
        jax.ShapeDtypeStruct((n, A_WIDTH), BF16),
        jax.ShapeDtypeStruct((n, A_WIDTH), F32),
        jax.ShapeDtypeStruct((n, MQZ_WIDTH), BF16),
        jax.ShapeDtypeStruct((n, LANES), F32),
        jax.ShapeDtypeStruct((groups, 8, tg), F32),
    )
    row = lambda w: pl.BlockSpec((tm, w), lambda i: (i, 0))
    return pl.pallas_call(
        _inproj_a_kernel,
        out_shape=outs,
        grid=(n // tm,),
        in_specs=[
            row(D_MODEL),
            pl.BlockSpec((1, D_MODEL), lambda i: (0, 0)),
            _resident((None, D_MODEL, wcols), lambda i: (l, 0, 0)),
            _resident((None, A_WIDTH, D_MODEL), lambda i: (l, 0, 0)),
            _resident((None, 8, D_MODEL), lambda i: (l, 0, 0)),
            pl.BlockSpec((None, 1, LANES), lambda i: (l, 0, 0)),
            pl.BlockSpec((None, 8, 1), lambda i: (l, 0, 0)),
        ],
        out_specs=(
            row(A_WIDTH),
            pl.BlockSpec((None, A_WIDTH, tm), lambda i: (i // per, 0, i % per)),
            row(A_WIDTH),
            row(A_WIDTH),
            row(MQZ_WIDTH),
            row(LANES),
            pl.BlockSpec((None, 8, tm), lambda i: (i // per, 0, i % per)),
        ),
        compiler_params=_params("arbitrary"),
        name="inproj_a",
    )(x, g, wrow, wkt, wgt, brow, bcol)


def _inproj_b_kernel(x_ref, g_ref, w_ref, qz_ref, mq_ref):
    xb = _rms(x_ref[...], g_ref[...]).astype(BF16)
    y = _dot(xb, w_ref[...])
    qz_ref[...] = (y[:, :QZ_WIDTH] * (DIFF_HEAD_DIM ** -0.5 * LOG2E)).astype(BF16)
    mq_ref[...] = (y[:, QZ_WIDTH:] * (MEM_HEAD_DIM ** -0.5 * LOG2E)).astype(BF16)


def _inproj_b(x, g, w, lb):
    n = x.shape[0]
    tm = min(TOKEN_TILE, n)
    return pl.pallas_call(
        _inproj_b_kernel,
        out_shape=(jax.ShapeDtypeStruct((n, QZ_WIDTH), BF16), jax.ShapeDtypeStruct((n, MQZ_WIDTH), BF16)),
        grid=(n // tm,),
        in_specs=[
            pl.BlockSpec((tm, D_MODEL), lambda i: (i, 0)),
            pl.BlockSpec((1, D_MODEL), lambda i: (0, 0)),
            _resident((None, D_MODEL, QZ_WIDTH + MQZ_WIDTH), lambda i: (lb, 0, 0)),
        ],
        out_specs=(pl.BlockSpec((tm, QZ_WIDTH), lambda i: (i, 0)), pl.BlockSpec((tm, MQZ_WIDTH), lambda i: (i, 0))),
        compiler_params=_params("arbitrary"),
        name="inproj_b",
    )(x, g, w)


def _kvproj_kernel(x_ref, g_ref, w_ref, k_ref, v_ref, kb_ref, vb_ref):
    xb = _rms(x_ref[...], g_ref[...]).astype(BF16)
    y = _dot(xb, w_ref[...])
    k = y[:, :MAIN_WIDTH]
    v = y[:, MAIN_WIDTH:]
    k_ref[...] = k
    v_ref[...] = v
    kb_ref[...] = k.astype(BF16)
    vb_ref[...] = v.astype(BF16)


def _kvproj(x, g, w):
    n = x.shape[0]
    tm = min(TOKEN_TILE, n)
    blk = pl.BlockSpec((tm, MAIN_WIDTH), lambda i: (i, 0))
    return pl.pallas_call(
        _kvproj_kernel,
        out_shape=(jax.ShapeDtypeStruct((n, MAIN_WIDTH), F32), jax.ShapeDtypeStruct((n, MAIN_WIDTH), F32),
                   jax.ShapeDtypeStruct((n, MAIN_WIDTH), BF16), jax.ShapeDtypeStruct((n, MAIN_WIDTH), BF16)),
        grid=(n // tm,),
        in_specs=[
            pl.BlockSpec((tm, D_MODEL), lambda i: (i, 0)),
            pl.BlockSpec((1, D_MODEL), lambda i: (0, 0)),
            _resident((D_MODEL, 2 * MAIN_WIDTH), lambda i: (0, 0)),
        ],
        out_specs=(blk, blk, blk, blk),
        compiler_params=_params("arbitrary"),
        name="kvproj",
    )(x, g, w)


def _kvproj_seq_kernel(x_ref, g_ref, w_ref, wkt_ref, kt_ref, v4_ref, kb_ref, vb_ref):
    xb = _rms(x_ref[...], g_ref[...]).astype(BF16)
    y = _dot(xb, w_ref[...])
    v = y[:, MAIN_WIDTH:]
    kb_ref[...] = y[:, :MAIN_WIDTH].astype(BF16)
    vb_ref[...] = v.astype(BF16)
    kt_ref[...] = _dot_nt(wkt_ref[...], xb)
    for h in range(DIFF_HEADS):
        v4_ref[h] = v[:, h * DIFF_V_DIM:(h + 1) * DIFF_V_DIM]


def _kvproj_seq(x, g, w, wkt, groups):
    n = x.shape[0]
    tg = n // groups
    tm = min(TOKEN_TILE, tg)
    per = tg // tm
    blk = pl.BlockSpec((tm, MAIN_WIDTH), lambda i: (i, 0))
    return pl.pallas_call(
        _kvproj_seq_kernel,
        out_shape=(jax.ShapeDtypeStruct((groups, MAIN_WIDTH, tg), F32),
                   jax.ShapeDtypeStruct((groups, DIFF_HEADS, tg, DIFF_V_DIM), F32),
                   jax.ShapeDtypeStruct((n, MAIN_WIDTH), BF16), jax.ShapeDtypeStruct((n, MAIN_WIDTH), BF16)),
        grid=(n // tm,),
        in_specs=[
            pl.BlockSpec((tm, D_MODEL), lambda i: (i, 0)),
            pl.BlockSpec((1, D_MODEL), lambda i: (0, 0)),
            _resident((D_MODEL, 2 * MAIN_WIDTH), lambda i: (0, 0)),
            _resident((MAIN_WIDTH, D_MODEL), lambda i: (0, 0)),
        ],
        out_specs=(pl.BlockSpec((None, MAIN_WIDTH, tm), lambda i: (i // per, 0, i % per)),
                   pl.BlockSpec((None, DIFF_HEADS, tm, DIFF_V_DIM), lambda i: (i // per, 0, i % per, 0)),
                   blk, blk),
        compiler_params=_params("arbitrary"),
        name="kvproj_seq",
    )(x, g, w, wkt)


def _memkv_kernel(m_ref, wt_ref, kt_ref, vt_ref):
    yt = _dot_nt(wt_ref[...], m_ref[...].astype(BF16))
    kt_ref[...] = yt[:MEM_WIDTH]
    vt_ref[...] = yt[MEM_WIDTH:]


def _memkv(mem, wt):
    b = mem.shape[0]
    out = jax.ShapeDtypeStruct((DEPTH, b, MEM_WIDTH, N_MEM), F32)
    blk = pl.BlockSpec((None, None, MEM_WIDTH, N_MEM), lambda l, i: (l, i, 0, 0))
    return pl.pallas_call(
        _memkv_kernel,
        out_shape=(out, out),
        grid=(DEPTH, b),
        in_specs=[pl.BlockSpec((None, N_MEM, D_MODEL), lambda l, i: (i, 0, 0)),
                  pl.BlockSpec((None, 2 * MEM_WIDTH, D_MODEL), lambda l, i: (l, 0, 0))],
        out_specs=(blk, blk),
        compiler_params=_params("arbitrary", "arbitrary"),
        name="memkv",
    )(mem, wt)


def _split3(x):
    hi = x.astype(BF16)
    r = x - hi.astype(F32)
    mid = r.astype(BF16)
    lo = (r - mid.astype(F32)).astype(BF16)
    return hi, mid, lo


def _mlstm_kernel(q_ref, kt_ref, v_ref, o_ref, gcol_ref, grow_ref, gh_ref, ct0_ref, m0_ref,
                  h_ref, ctn_ref, mn_ref, ct_s, m_s, tri_s, *, sequential, chunk):
    t = pl.program_id(1)
    nb, tb = q_ref.shape[0], q_ref.shape[1]
    win = max(chunk, LANES)
    units = [(b, c * chunk) for b in range(nb) for c in range(tb // chunk)]

    @pl.when((pl.program_id(0) == 0) & (t == 0))
    def _init_tri():
        r = lax.broadcasted_iota(jnp.int32, (tb, tb), 0)
        c = lax.broadcasted_iota(jnp.int32, (tb, tb), 1)
        tri_s[...] = jnp.where((r // chunk == c // chunk) & (c <= r), 1.0, 0.0).astype(BF16)

    @pl.when((t == 0) | (not sequential))
    def _init_state():
        ct_s[...] = ct0_ref[...]
        m_s[...] = m0_ref[...]

    tri = tri_s[...]
    bcols = [sum(_dot(tri, part) for part in _split3(gcol_ref[b])) for b in range(nb)]
    brows = [sum(_dot_nt(part, tri) for part in _split3(grow_ref[b])) for b in range(nb)]

    row_i = lax.broadcasted_iota(jnp.int32, (chunk, win), 0)
    lane_i = lax.broadcasted_iota(jnp.int32, (chunk, win), 1)
    lane_row = lax.broadcasted_iota(jnp.int32, (1, win), 1)
    pad_lane = lax.broadcasted_iota(jnp.int32, (chunk, HEAD_PAD), 1)

    pairs = []
    for slot, (b, r0) in enumerate(units):
        p0 = r0 // win * win
        off = r0 - p0
        in_chunk = (lane_i >= off) & (lane_i < off + chunk)
        vis = in_chunk & (lane_i - off <= row_i)
        in_chunk_row = (lane_row >= off) & (lane_row < off + chunk)
        for h in range(MLSTM_HEADS):
            pairs.append(dict(slot=slot, b=b, r0=r0, p0=p0, off=off, h=h, vis=vis, in_chunk_row=in_chunk_row,
                              hs=slice(h * HEAD_PAD, (h + 1) * HEAD_PAD)))

    for p in pairs:
        fl = MLSTM_HEADS + p["h"]
        bcol = bcols[p["b"]][p["r0"]:p["r0"] + chunk, fl:fl + 1]
        p["brow"] = brows[p["b"]][fl:fl + 1, p["p0"]:p["p0"] + win]
        p["igrow"] = grow_ref[p["b"], p["h"]:p["h"] + 1, p["p0"]:p["p0"] + win]
        p["m_prev"] = m_s[p["slot"], p["h"]:p["h"] + 1, 0:1]
        dlog = jnp.where(p["vis"], bcol - p["brow"] + p["igrow"], -jnp.inf)
        a = jnp.max(dlog, axis=1, keepdims=True)
        p["mt"] = jnp.maximum(bcol + p["m_prev"], a)
        p["e"] = jnp.exp(dlog - p["mt"])
        p["w_inter"] = jnp.exp(bcol + p["m_prev"] - p["mt"])

    for p in pairs:
        b, r0, p0, hs = p["b"], p["r0"], p["p0"], p["hs"]
        qh = q_ref[b, r0:r0 + chunk, hs]
        p["kth"] = kt_ref[b, hs, p0:p0 + win]
        p["vh"] = v_ref[b, p0:p0 + win, hs]
        s = (_dot(qh, p["kth"]) * p["e"]).astype(BF16)
        p["ct"] = ct_s[p["slot"], p["h"]]
        p["num"] = p["w_inter"] * _dot(qh, p["ct"].astype(BF16)) + _dot(s, p["vh"])

    for p in pairs:
        b, r0, hs, num = p["b"], p["r0"], p["hs"], p["num"]
        nq = num[:, N_COL:N_COL + 1]
        den = jnp.maximum(jnp.abs(nq), jnp.exp(-p["mt"]))
        hh = jnp.where(pad_lane < MLSTM_HEAD_DIM, num / den, 0.0)
        ms = jnp.sum(hh * hh, axis=1, keepdims=True) * (1.0 / MLSTM_HEAD_DIM)
        hn = hh * lax.rsqrt(ms + EPS) * gh_ref[:, hs]
        h_ref[b, r0:r0 + chunk, hs] = (hn * jax.nn.sigmoid(o_ref[b, r0:r0 + chunk, hs])).astype(BF16)

    for p in pairs:
        last = p["off"] + chunk - 1
        b_last = p["brow"][:, last:last + 1]
        m_new = p["mt"][chunk - 1:chunk, :]
        w_end = jnp.where(p["in_chunk_row"], jnp.exp(b_last - p["brow"] + p["igrow"] - m_new), 0.0)
        decay = jnp.exp(b_last + p["m_prev"] - m_new)
        kw = (p["kth"].astype(F32) * w_end).astype(BF16)
        ct_s[p["slot"], p["h"]] = decay * p["ct"] + _dot(kw, p["vh"])
        m_s[p["slot"], p["h"]:p["h"] + 1, :] = jnp.broadcast_to(m_new, (1, LANES))

    @pl.when((t == pl.num_programs(1) - 1) | (not sequential))
    def _fin():
        ctn_ref[...] = ct_s[...]
        mn_ref[...] = m_s[...]


def _mlstm(q, kt, v, o, gcol, grow, ghead, ct0, m0, l, groups, sequential):
    n = q.shape[0]
    tg = n // groups
    if sequential:
        chunk = tb = min(MLSTM_SEQ_CHUNK, tg)
        nb = min(MLSTM_GROUPS_PER_STEP, groups)
    else:
        chunk, tb, nb = CHUNK, min(MLSTM_TILE, tg), 1
    steps = tg // tb
    slots = nb * (tb // chunk)
    streams = ct0.shape[1]
    sblk = (lambda g, t: g) if sequential else (lambda g, t: g * steps + t)
    rows3 = lambda a: a.reshape(groups, tg, a.shape[-1])
    row = lambda w: pl.BlockSpec((nb, tb, w), lambda g, t: (g, t, 0))
    col = lambda r: pl.BlockSpec((nb, r, tb), lambda g, t: (g, 0, t))
    kernel = functools.partial(_mlstm_kernel, sequential=sequential, chunk=chunk)
    h, ctn, mn = pl.pallas_call(
        kernel,
        out_shape=(jax.ShapeDtypeStruct((groups, tg, A_WIDTH), BF16),
                   jax.ShapeDtypeStruct((streams, MLSTM_HEADS, HEAD_PAD, HEAD_PAD), F32),
                   jax.ShapeDtypeStruct((streams, 8, LANES), F32)),
        grid=(groups // nb, steps),
        in_specs=[
            row(A_WIDTH),
            col(A_WIDTH),
            row(A_WIDTH),
            row(A_WIDTH),
            row(LANES),
            col(8),
            pl.BlockSpec((None, 1, A_WIDTH), lambda g, t: (l, 0, 0)),
            pl.BlockSpec((None, slots, MLSTM_HEADS, HEAD_PAD, HEAD_PAD), lambda g, t: (l, sblk(g, t), 0, 0, 0)),
            pl.BlockSpec((None, slots, 8, LANES), lambda g, t: (l, sblk(g, t), 0, 0)),
        ],
        out_specs=(
            row(A_WIDTH),
            pl.BlockSpec((slots, MLSTM_HEADS, HEAD_PAD, HEAD_PAD), lambda g, t: (sblk(g, t), 0, 0, 0)),
            pl.BlockSpec((slots, 8, LANES), lambda g, t: (sblk(g, t), 0, 0)),
        ),
        scratch_shapes=[
            pltpu.VMEM((slots, MLSTM_HEADS, HEAD_PAD, HEAD_PAD), F32),
            pltpu.VMEM((slots, 8, LANES), F32),
            pltpu.VMEM((tb, tb), BF16),
        ],
        compiler_params=_params("arbitrary", "arbitrary"),
        name="mlstm",
    )(rows3(q), kt, rows3(v), rows3(o), rows3(gcol), grow, ghead, ct0, m0)
    return h.reshape(n, A_WIDTH), ctn, mn


_SLOPES2 = tuple(LOG2E * 2.0 ** (-8.0 * (h + 1) / DIFF_HEADS) for h in range(DIFF_HEADS))


def _lam(lam_ref, lam_init):
    lp = lam_ref[...]
    s1 = jnp.sum(lp[0:1] * lp[1:2], axis=1, keepdims=True)
    s2 = jnp.sum(lp[2:3] * lp[3:4], axis=1, keepdims=True)
    return jnp.exp(s1) - jnp.exp(s2) + lam_init


def _vext(v_tile):
    return jnp.concatenate([v_tile, jnp.ones_like(v_tile)], axis=1)


def _finish_head(acc1, acc2, lam, gsub, lam_init):
    o = acc1[:, :DIFF_V_DIM] / acc1[:, DIFF_V_DIM:] - lam * (acc2[:, :DIFF_V_DIM] / acc2[:, DIFF_V_DIM:])
    return (_rms(o, gsub) * (1.0 - lam_init)).astype(BF16)


def _attn_prompt_kernel(it_ref, jt_ref, qz_ref, k_ref, v_ref, lam_ref, gsub_ref, o_ref, acc_s, m_s, *, lam_init, tq):
    step_id = pl.program_id(1)
    i = it_ref[step_id]
    j = jt_ref[step_id]
    tk = 2 * tq
    last = j == (i >> 1)
    odd = (i & 1) == 1

    @pl.when(j == 0)
    def _init():
        acc_s[...] = jnp.zeros_like(acc_s)
        m_s[...] = jnp.full_like(m_s, -jnp.inf)

    def step(keys, bias_fn):
        def scores(idx):
            h = idx % DIFF_HEADS
            qt = qz_ref[:, idx * LANES:(idx + 1) * LANES]
            kt_i = idx // DIFF_HEADS * (DIFF_HEADS // 2) + h // 2
            kt = k_ref[0:keys, kt_i * LANES:(kt_i + 1) * LANES]
            return bias_fn(_dot_nt(qt, kt), h)

        s_next = scores(0)
        for idx in range(2 * DIFF_HEADS):
            s = s_next
            if idx + 1 < 2 * DIFF_HEADS:
                s_next = scores(idx + 1)
            h = idx % DIFF_HEADS
            m_prev = m_s[idx]
            m_new = jnp.maximum(m_prev, jnp.max(s, axis=1, keepdims=True))
            p = jnp.exp2(s - jnp.concatenate([m_new] * (keys // LANES), axis=1)).astype(BF16)
            alpha = jnp.exp2(m_prev - m_new)
            vt = _vext(v_ref[0:keys, h * LANES:(h + 1) * LANES])
            acc_s[idx] = jnp.concatenate([alpha, alpha], axis=1) * acc_s[idx] + _dot(p, vt)
            m_s[idx] = m_new

    def finish():
        lam = _lam(lam_ref, lam_init)
        for h in range(DIFF_HEADS):
            o_ref[:, h * LANES:(h + 1) * LANES] = _finish_head(
                acc_s[h], acc_s[DIFF_HEADS + h], lam, gsub_ref[...], lam_init)

    def own_keys(keys):
        r = lax.broadcasted_iota(jnp.int32, (tq, keys), 0)
        c = lax.broadcasted_iota(jnp.int32, (tq, keys), 1) - (keys - tq)
        vis = (c < 0) | (jnp.maximum(c, 0) // CHUNK <= r // CHUNK)
        g = (r - jnp.abs(r - c)).astype(F32)
        step(keys, lambda s, h: jnp.where(vis, s + _SLOPES2[h] * g, -jnp.inf))
        finish()

    @pl.when(jnp.logical_not(last))
    def _past():
        col = lax.broadcasted_iota(jnp.int32, (1, tk), 1).astype(F32)
        off = (j * tk - i * tq).astype(F32)
        step(tk, lambda s, h: s + _SLOPES2[h] * (col + off))

    @pl.when(last & odd)
    def _diag_odd():
        own_keys(tk)

    @pl.when(last & jnp.logical_not(odd))
    def _diag_even():
        own_keys(tq)


def _attn_prompt(qz, kb, vb, lambdas, gsub, lam_init, groups):
    n = qz.shape[0]
    tg = n // groups
    tq = min(ATTN_TQ, tg // 2)
    tk = 2 * tq
    nq = tg // tq
    nkv = tg // tk
    steps = [(i, j) for i in range(nq) for j in range(i // 2 + 1)]
    itab = jnp.asarray([s[0] for s in steps], jnp.int32)
    jtab = jnp.asarray([s[1] for s in steps], jnp.int32)
    kernel = functools.partial(_attn_prompt_kernel, lam_init=lam_init, tq=tq)
    q_spec = lambda w: pl.BlockSpec((tq, w), lambda b, s, it, jt: (b * nq + it[s], 0))
    kv_spec = pl.BlockSpec((tk, MAIN_WIDTH), lambda b, s, it, jt: (b * nkv + jt[s], 0))
    grid_spec = pltpu.PrefetchScalarGridSpec(
        num_scalar_prefetch=2,
        grid=(groups, len(steps)),
        in_specs=[
            q_spec(QZ_WIDTH),
            kv_spec,
            kv_spec,
            pl.BlockSpec((4, DIFF_HEAD_DIM), lambda b, s, it, jt: (0, 0)),
            pl.BlockSpec((1, DIFF_V_DIM), lambda b, s, it, jt: (0, 0)),
        ],
        out_specs=q_spec(MAIN_WIDTH),
        scratch_shapes=[
            pltpu.VMEM((2 * DIFF_HEADS, tq, 2 * DIFF_V_DIM), F32),
            pltpu.VMEM((2 * DIFF_HEADS, tq, LANES), F32),
        ],
    )
    return pl.pallas_call(
        kernel,
        out_shape=jax.ShapeDtypeStruct((n, MAIN_WIDTH), BF16),
        grid_spec=grid_spec,
        compiler_params=_params("arbitrary", "arbitrary"),
        name="attn_prompt",
    )(itab, jtab, qz, kb, vb, lambdas, gsub)


def _attn_sample_kernel(qz_ref, ckt_ref, cv_ref, nk_ref, nv_ref, lam_ref, gsub_ref, o_ref, *, lam_init, past):
    tq = qz_ref.shape[0]

    def geometry(width, c0):
        r = lax.broadcasted_iota(jnp.int32, (tq, width), 0) + past
        c = lax.broadcasted_iota(jnp.int32, (tq, width), 1) + c0
        return c // CHUNK <= r // CHUNK, jnp.abs(r - c).astype(F32)

    vis_c, dist_c = geometry(past, 0)
    vis_n, dist_n = geometry(tq, past)
    lam = _lam(lam_ref, lam_init)
    probs = {}
    for kt_i in range(DIFF_HEADS):
        a, h0 = kt_i // (DIFF_HEADS // 2), 2 * (kt_i % (DIFF_HEADS // 2))
        ks = slice(kt_i * LANES, (kt_i + 1) * LANES)
        q2 = jnp.concatenate([qz_ref[:, (a * DIFF_HEADS + h) * LANES:(a * DIFF_HEADS + h + 1) * LANES]
                              for h in (h0, h0 + 1)], axis=0)
        sc2 = _dot(q2, ckt_ref[ks, :].astype(BF16))
        sn2 = _dot_nt(q2, nk_ref[:, ks])
        for r, h in enumerate((h0, h0 + 1)):
            rows = slice(r * tq, (r + 1) * tq)
            s_c = jnp.where(vis_c, sc2[rows] - _SLOPES2[h] * dist_c, -jnp.inf)
            s_n = jnp.where(vis_n, sn2[rows] - _SLOPES2[h] * dist_n, -jnp.inf)
            m = jnp.maximum(jnp.max(s_c, axis=1, keepdims=True), jnp.max(s_n, axis=1, keepdims=True))
            probs[a, h] = (jnp.exp2(s_c - m).astype(BF16), jnp.exp2(s_n - m).astype(BF16))
    for h in range(DIFF_HEADS):
        p_c = jnp.concatenate([probs[0, h][0], probs[1, h][0]], axis=0)
        p_n = jnp.concatenate([probs[0, h][1], probs[1, h][1]], axis=0)
        acc = (_dot(p_c, _vext(cv_ref[h].astype(BF16)))
               + _dot(p_n, _vext(nv_ref[:, h * LANES:(h + 1) * LANES])))
        o_ref[:, h * LANES:(h + 1) * LANES] = _finish_head(acc[0:tq], acc[tq:2 * tq], lam, gsub_ref[...], lam_init)


def _attn_sample(qz, cache_kt, cache_v, kb, vb, lambdas, gsub, lam_init):
    streams, _, past = cache_kt.shape
    n = qz.shape[0]
    tq = n // streams
    kernel = functools.partial(_attn_sample_kernel, lam_init=lam_init, past=past)
    new = pl.BlockSpec((tq, MAIN_WIDTH), lambda b: (b, 0))
    return pl.pallas_call(
        kernel,
        out_shape=jax.ShapeDtypeStruct((n, MAIN_WIDTH), BF16),
        grid=(streams,),
        in_specs=[
            pl.BlockSpec((tq, QZ_WIDTH), lambda b: (b, 0)),
            pl.BlockSpec((None, MAIN_WIDTH, past), lambda b: (b, 0, 0)),
            pl.BlockSpec((None, DIFF_HEADS, past, DIFF_V_DIM), lambda b: (b, 0, 0, 0)),
            new, new,
            pl.BlockSpec((4, DIFF_HEAD_DIM), lambda b: (0, 0)),
            pl.BlockSpec((1, DIFF_V_DIM), lambda b: (0, 0)),
        ],
        out_specs=new,
        compiler_params=_params("arbitrary"),
        name="attn_sample",
    )(qz, cache_kt, cache_v, kb, vb, lambdas, gsub)


def _memattn_kernel(mq_ref, mkt_ref, mvt_ref, o_ref, *, tq):
    lane = lax.broadcasted_iota(jnp.int32, (tq, LANES), 1)
    for i in range(mkt_ref.shape[0]):
        rows = slice(i * tq, (i + 1) * tq)
        for pair in range(MEM_HEADS // 2):
            ks = slice(pair * LANES, (pair + 1) * LANES)
            mkt = mkt_ref[i, ks, :].astype(BF16)
            mvt = mvt_ref[i, ks, :].astype(BF16)
            outs = []
            for h in (2 * pair, 2 * pair + 1):
                s = _dot(mq_ref[rows, h * LANES:(h + 1) * LANES], mkt)
                p = jnp.exp2(s - jnp.max(s, axis=1, keepdims=True))
                l = jnp.sum(p, axis=1, keepdims=True)
                outs.append(_dot_nt(p.astype(BF16), mvt) / l)
            o_ref[rows, ks] = jnp.where(lane < MEM_HEAD_DIM, outs[0], outs[1]).astype(BF16)


def _memattn(mq, mkt, mvt, l):
    n = mq.shape[0]
    streams = mkt.shape[1]
    tg = n // streams
    tq = min(MEM_TILE, tg)
    per = tg // tq
    ns = MEM_TILE // tq
    rows = ns * tq
    mem = pl.BlockSpec((None, ns, MEM_WIDTH, N_MEM), lambda i: (l, i // per, 0, 0))
    return pl.pallas_call(
        functools.partial(_memattn_kernel, tq=tq),
        out_shape=jax.ShapeDtypeStruct((n, MEM_WIDTH), BF16),
        grid=(n // rows,),
        in_specs=[pl.BlockSpec((rows, MQZ_WIDTH), lambda i: (i, 0)), mem, mem],
        out_specs=pl.BlockSpec((rows, MEM_WIDTH), lambda i: (i, 0)),
        compiler_params=_params("arbitrary"),
        name="memattn",
    )(mq, mkt, mvt)


def _pad_heads(w, heads, dim, pad):
    lead = w.shape[:-1]
    w = w.reshape(lead + (heads, dim))
    w = jnp.pad(w, [(0, 0)] * len(lead) + [(0, 0), (0, pad - dim)])
    return w.reshape(lead + (heads * pad,))


def _interleave64(w, heads):
    lead = w.shape[:-1]
    w = w.reshape(lead + (heads, 64))
    z = jnp.zeros_like(w)
    even = jnp.concatenate([w, z], axis=-1)
    odd = jnp.concatenate([z, w], axis=-1)
    sel = (jnp.arange(heads) % 2 == 0)[:, None]
    return jnp.where(sel, even, odd).reshape(lead + (heads * LANES,))


def _prep_weights(w_up, w_down, norm_g, w_in_a, b_gates, g_head_a, w_in_b, w_kv, w_mem_kv, w_out):
    mw = MAIN_WIDTH
    prep = {
        "w_up": w_up.astype(BF16),
        "w_down": w_down.astype(BF16),
        "norm_g": norm_g.reshape(DEPTH, 6, 1, D_MODEL),
        "w_kv": w_kv.astype(BF16),
        "w_kt": w_kv[:, :mw].T.astype(BF16),
        "w_mem_kv_t": jnp.swapaxes(w_mem_kv, 1, 2).astype(BF16),
        "w_out_mem": w_out[:, mw:, :].astype(BF16),
        "w_out_b": w_out[N_A_LAYERS:, :mw, :].astype(BF16),
    }
    wa = w_out[:N_A_LAYERS, :mw, :].reshape(N_A_LAYERS, MLSTM_HEADS, MLSTM_HEAD_DIM, D_MODEL)
    wa = jnp.pad(wa, ((0, 0), (0, 0), (0, HEAD_PAD - MLSTM_HEAD_DIM), (0, 0)))
    prep["w_out_a"] = wa.reshape(N_A_LAYERS, A_WIDTH, D_MODEL).astype(BF16)
    pad_a = lambda w: _pad_heads(w, MLSTM_HEADS, MLSTM_HEAD_DIM, HEAD_PAD)
    q, k, v, o = (pad_a(w_in_a[:, :, s * mw:(s + 1) * mw]) for s in range(4))
    gates = w_in_a[:, :, 4 * mw:4 * mw + 2 * MLSTM_HEADS]
    mq = _interleave64(w_in_a[:, :, 4 * mw + 2 * MLSTM_HEADS:], MEM_HEADS)
    gates_pad = jnp.pad(gates, ((0, 0), (0, 0), (0, LANES - 2 * MLSTM_HEADS)))
    prep["wrow_a"] = jnp.concatenate([q, v, o, mq, gates_pad], axis=-1).astype(BF16)
    prep["wkt_a"] = jnp.swapaxes(k, 1, 2).astype(BF16)
    prep["wgt_a"] = jnp.swapaxes(gates, 1, 2).astype(BF16)
    prep["brow_a"] = jnp.pad(b_gates, ((0, 0), (0, LANES - 2 * MLSTM_HEADS))).reshape(N_A_LAYERS, 1, LANES)
    prep["bcol_a"] = b_gates.reshape(N_A_LAYERS, 2 * MLSTM_HEADS, 1)
    prep["ghead_a"] = jnp.pad(g_head_a, ((0, 0), (0, 0), (0, HEAD_PAD - MLSTM_HEAD_DIM))).reshape(N_A_LAYERS, 1, A_WIDTH)
    nq = 2 * DIFF_HEADS * DIFF_HEAD_DIM
    prep["w_in_b"] = jnp.concatenate(
        [_interleave64(w_in_b[:, :, :nq], 2 * DIFF_HEADS), _interleave64(w_in_b[:, :, nq:], MEM_HEADS)],
        axis=-1).astype(BF16)
    return prep


def _trunk(x, groups, mem_k, mem_v, ct0, m0, cache_k, cache_v, W, lambdas, g_sub_b, g_kv):
    prompt = cache_k is None
    g = W["norm_g"]
    states = []
    k_new = v_new = kb = vb = None
    for l in range(DEPTH):
        x = _ffn(x, g[l, 0], W["w_up"], W["w_down"], g[l, 1], l, 0)
        if l < N_A_LAYERS:
            q, kt, v, o, mq, gcol, grow = _inproj_a(
                x, g[l, 2], W["wrow_a"], W["wkt_a"], W["wgt_a"], W["brow_a"], W["bcol_a"], l, groups)
            main, ctn, mn = _mlstm(q, kt, v, o, gcol, grow, W["ghead_a"], ct0, m0, l, groups, prompt)
            states.append((ctn, mn))
            w_main, lm = W["w_out_a"], l
        else:
            lb = l - N_A_LAYERS
            lam_init = 0.8 - 0.6 * math.exp(-0.3 * l)
            qz, mq = _inproj_b(x, g[l, 2], W["w_in_b"], lb)
            if prompt:
                main = _attn_prompt(qz, kb, vb, lambdas[lb], g_sub_b[lb], lam_init, groups)
            else:
                main = _attn_sample(qz, cache_k, cache_v, kb, vb, lambdas[lb], g_sub_b[lb], lam_init)
            w_main, lm = W["w_out_b"], lb
        mem = _memattn(mq, mem_k, mem_v, l)
        x = _mix_ffn(main, mem, x, w_main, lm, W["w_out_mem"], g[l, 3], g[l, 4], W["w_up"], W["w_down"], g[l, 5], l)
        if l == N_A_LAYERS - 1:
            if prompt:
                k_new, v_new, kb, vb = _kvproj_seq(x, g_kv, W["w_kv"], W["w_kt"], groups)
            else:
                k_new, v_new, kb, vb = _kvproj(x, g_kv, W["w_kv"])
    return x, states, k_new, v_new


def _unpack_states(states, lead):
    hd = MLSTM_HEAD_DIM
    ct = jnp.stack([s[0] for s in states])
    m = jnp.stack([s[1] for s in states])
    c = jnp.swapaxes(ct[..., :hd, :hd], -1, -2)
    n = ct[..., :hd, N_COL]
    return c, n, m[..., :MLSTM_HEADS, 0]


def kernel(x_prompt, x_sample, mem_prompt, cache_mem_k, cache_mem_v, state_C, state_n, state_m, cache_k, cache_v,
           w_up, w_down, norm_g, w_in_a, b_gates, g_head_a, w_in_b, lambdas, g_sub_b, g_kv, w_kv, w_mem_kv, w_out):
    W = _prep_weights(w_up, w_down, norm_g, w_in_a, b_gates, g_head_a, w_in_b, w_kv, w_mem_kv, w_out)
    g_kv2 = g_kv.reshape(1, D_MODEL)
    g_sub = g_sub_b.reshape(-1, 1, DIFF_V_DIM)
    hd = MLSTM_HEAD_DIM

    bp, tp, _ = x_prompt.shape
    mkt_p, mvt_p = _memkv(mem_prompt, W["w_mem_kv_t"])
    ct0 = jnp.zeros((N_A_LAYERS, bp, MLSTM_HEADS, HEAD_PAD, HEAD_PAD), F32)
    m0 = jnp.zeros((N_A_LAYERS, bp, 8, LANES), F32)
    y_p, st_p, kt_p, v4_p = _trunk(x_prompt.reshape(bp * tp, D_MODEL), bp, mkt_p, mvt_p, ct0, m0, None, None,
                                   W, lambdas, g_sub, g_kv2)
    c_p, n_p, m_p = _unpack_states(st_p, bp)

    bs, ts, _ = x_sample.shape
    pad = HEAD_PAD - hd
    ct_s = jnp.concatenate([jnp.swapaxes(state_C, -1, -2), state_n[..., None],
                            jnp.zeros(state_n.shape + (pad - 1,), F32)], axis=-1)
    ct_s = jnp.pad(ct_s, [(0, 0)] * 3 + [(0, pad), (0, 0)])
    m_s = jnp.broadcast_to(jnp.pad(state_m, ((0, 0), (0, 0), (0, 8 - MLSTM_HEADS)))[..., None],
                           (N_A_LAYERS, bs, 8, LANES))
    mem_t = lambda a: jnp.transpose(a, (0, 1, 3, 4, 2)).reshape(DEPTH, -1, MEM_WIDTH, N_MEM)
    y_s, st_s, k_s, v_s = _trunk(
        x_sample.reshape(bs * ts, D_MODEL), 1, mem_t(cache_mem_k), mem_t(cache_mem_v), ct_s, m_s,
        jnp.transpose(cache_k, (0, 2, 3, 1)).reshape(bs, MAIN_WIDTH, -1), jnp.transpose(cache_v, (0, 2, 1, 3)),
        W, lambdas, g_sub, g_kv2)
    c_s, n_s, m_sn = _unpack_states(st_s, bs)

    kshape = (2 * DIFF_HEADS, DIFF_HEAD_DIM)
    vshape = (DIFF_HEADS, DIFF_V_DIM)
    mem_out = lambda a: jnp.transpose(a.reshape(DEPTH, bp, MEM_HEADS, MEM_HEAD_DIM, N_MEM), (0, 1, 4, 2, 3))
    return (y_p.reshape(bp, tp, D_MODEL), y_s.reshape(bs, ts, D_MODEL),
            mem_out(mkt_p), mem_out(mvt_p),
            c_p, n_p, m_p,
            jnp.transpose(kt_p.reshape((bp,) + kshape + (tp,)), (0, 3, 1, 2)), jnp.transpose(v4_p, (0, 2, 1, 3)),
            c_s, n_s, m_sn,
            k_s.reshape((bs, ts) + kshape), v_s.reshape((bs, ts) + vshape))
```

```python
import functools
import math

import jax
import jax.numpy as jnp
from jax import lax
from jax.experimental import pallas as pl
from jax.experimental.pallas import tpu as pltpu

F32 = jnp.float32
BF16 = jnp.bfloat16

D_MODEL = 1024
DEPTH = 4
CHUNK = 64
N_MEM = 256
N_A_LAYERS = DEPTH // 2
MEM_HEADS = 4
MEM_WIDTH = 256
MEM_HEAD_DIM = 64
MAIN_WIDTH = 768
MLSTM_HEADS = 4
MLSTM_HEAD_DIM = 192
DIFF_HEAD_DIM = 64
DIFF_V_DIM = 128
DIFF_HEADS = 6
FFN_DIM = 2816
EPS = 1e-6

LANES = 128
HEAD_PAD = 256
N_COL = MLSTM_HEAD_DIM
A_WIDTH = MLSTM_HEADS * HEAD_PAD
QZ_WIDTH = 2 * DIFF_HEADS * LANES
MQZ_WIDTH = MEM_HEADS * LANES
LOG2E = math.log2(math.e)
VMEM_LIMIT = 56 * 1024 * 1024

TOKEN_TILE = 512
FFN_TILE = 1024
FFN_GROUP_ROWS = 256
MLSTM_TILE = 512
MLSTM_SEQ_CHUNK = 256
MLSTM_GROUPS_PER_STEP = 4
ATTN_TQ = 512
MEM_TILE = 512


def _params(*sem):
    return pltpu.CompilerParams(dimension_semantics=sem, vmem_limit_bytes=VMEM_LIMIT)


def _rms(x, g):
    return x * lax.rsqrt(jnp.mean(x * x, axis=-1, keepdims=True) + EPS) * g


def _log_sigmoid(x):
    return jnp.minimum(x, 0.0) - jnp.log1p(jnp.exp(-jnp.abs(x)))


def _dot(a, b):
    return jnp.dot(a, b, preferred_element_type=F32)


def _dot_nt(a, b):
    return lax.dot_general(a, b, (((1,), (1,)), ((), ())), preferred_element_type=F32)


def _resident(shape, index_map):
    return pl.BlockSpec(shape, index_map, pipeline_mode=pl.Buffered(1))


def _ffn_rows(x, g0_ref, wg_ref, wu_ref, wd_ref, g1_ref):
    xb = _rms(x, g0_ref[...]).astype(BF16)
    hg = _dot(xb, wg_ref[...])
    hu = _dot(xb, wu_ref[...])
    a = (hg * jax.nn.sigmoid(hg) * hu).astype(BF16)
    y = _dot(a, wd_ref[...])
    return x + 0.5 * _rms(y, g1_ref[...])


def _row_groups(tm):
    sub = min(FFN_GROUP_ROWS, tm)
    return [slice(r, r + sub) for r in range(0, tm, sub)]


def _ffn_kernel(x_ref, g0_ref, wg_ref, wu_ref, wd_ref, g1_ref, o_ref):
    for rows in _row_groups(x_ref.shape[0]):
        o_ref[rows, :] = _ffn_rows(x_ref[rows, :], g0_ref, wg_ref, wu_ref, wd_ref, g1_ref)


def _mix_ffn_kernel(main_ref, mem_ref, x_ref, wm_ref, we_ref, g3_ref, g0_ref, wg_ref, wu_ref, wd_ref, g1_ref, o_ref):
    groups = _row_groups(x_ref.shape[0])
    xs = []
    for rows in groups:
        y = _dot(main_ref[rows, :], wm_ref[...]) + _dot(mem_ref[rows, :], we_ref[...])
        xs.append(x_ref[rows, :] + _rms(y, g3_ref[...]))
    for rows, x in zip(groups, xs):
        o_ref[rows, :] = _ffn_rows(x, g0_ref, wg_ref, wu_ref, wd_ref, g1_ref)


def _mix_ffn(main, mem, x, w_main, lm, w_mem, g3, g0, w_up, w_down, g1, l):
    n = x.shape[0]
    tm = min(FFN_TILE, n)
    wm = main.shape[1]
    row = lambda w: pl.BlockSpec((tm, w), lambda i: (i, 0))
    gain = pl.BlockSpec((1, D_MODEL), lambda i: (0, 0))
    return pl.pallas_call(
        _mix_ffn_kernel,
        out_shape=jax.ShapeDtypeStruct((n, D_MODEL), F32),
        grid=(n // tm,),
        in_specs=[
            row(wm), row(MEM_WIDTH), row(D_MODEL),
            _resident((None, wm, D_MODEL), lambda i: (lm, 0, 0)),
            _resident((None, MEM_WIDTH, D_MODEL), lambda i: (l, 0, 0)),
            gain, gain,
            _resident((None, None, D_MODEL, FFN_DIM), lambda i: (l, 1, 0, 0)),
            _resident((None, None, D_MODEL, FFN_DIM), lambda i: (l, 1, 0, 1)),
            _resident((None, None, FFN_DIM, D_MODEL), lambda i: (l, 1, 0, 0)),
            gain,
        ],
        out_specs=row(D_MODEL),
        compiler_params=_params("arbitrary"),
        name="mix_ffn",
    )(main, mem, x, w_main, w_mem, g3, g0, w_up, w_up, w_down, g1)


def _ffn(x, g0, w_up, w_down, g1, l, k):
    n = x.shape[0]
    tm = min(FFN_TILE, n)
    return pl.pallas_call(
        _ffn_kernel,
        out_shape=jax.ShapeDtypeStruct((n, D_MODEL), F32),
        grid=(n // tm,),
        in_specs=[
            pl.BlockSpec((tm, D_MODEL), lambda i: (i, 0)),
            pl.BlockSpec((1, D_MODEL), lambda i: (0, 0)),
            _resident((None, None, D_MODEL, FFN_DIM), lambda i: (l, k, 0, 0)),
            _resident((None, None, D_MODEL, FFN_DIM), lambda i: (l, k, 0, 1)),
            _resident((None, None, FFN_DIM, D_MODEL), lambda i: (l, k, 0, 0)),
            pl.BlockSpec((1, D_MODEL), lambda i: (0, 0)),
        ],
        out_specs=pl.BlockSpec((tm, D_MODEL), lambda i: (i, 0)),
        compiler_params=_params("arbitrary"),
        name="ffn",
    )(x, g0, w_up, w_up, w_down, g1)


def _inproj_a_kernel(x_ref, g_ref, wrow_ref, wkt_ref, wgt_ref, brow_ref, bcol_ref,
                     q_ref, kt_ref, v_ref, o_ref, mq_ref, gcol_ref, grow_ref):
    for rows in _row_groups(x_ref.shape[0]):
        xb = _rms(x_ref[rows, :], g_ref[...]).astype(BF16)
        y = _dot(xb, wrow_ref[...])
        q_ref[rows, :] = y[:, 0:A_WIDTH].astype(BF16)
        v = y[:, A_WIDTH:2 * A_WIDTH]
        lane = lax.broadcasted_iota(jnp.int32, v.shape, 1)
        v_ref[rows, :] = jnp.where(lane % HEAD_PAD == N_COL, 1.0, v).astype(BF16)
        o_ref[rows, :] = y[:, 2 * A_WIDTH:3 * A_WIDTH]
        mq_ref[rows, :] = (y[:, 3 * A_WIDTH:3 * A_WIDTH + MQZ_WIDTH] * (MEM_HEAD_DIM ** -0.5 * LOG2E)).astype(BF16)
        g = y[:, 3 * A_WIDTH + MQZ_WIDTH:] + brow_ref[...]
        glane = lax.broadcasted_iota(jnp.int32, g.shape, 1)
        gcol_ref[rows, :] = jnp.where(glane < MLSTM_HEADS, g, _log_sigmoid(g))
        kt = _dot_nt(wkt_ref[...], xb)
        kt_ref[:, rows] = (kt * (MLSTM_HEAD_DIM ** -0.5)).astype(BF16)
        gt = _dot_nt(wgt_ref[...], xb) + bcol_ref[...]
        grow_idx = lax.broadcasted_iota(jnp.int32, gt.shape, 0)
        grow_ref[:, rows] = jnp.where(grow_idx < MLSTM_HEADS, gt, _log_sigmoid(gt))


def _inproj_a(x, g, wrow, wkt, wgt, brow, bcol, l, groups):
    n = x.shape[0]
    tg = n // groups
    tm = min(FFN_TILE, tg)
    per = tg // tm
    wcols = wrow.shape[2]
    outs = (
        jax.ShapeDtypeStruct((n, A_WIDTH), BF16),
        jax.ShapeDtypeStruct((groups, A_WIDTH, tg), BF16),
        jax.ShapeDtypeStruct((n, A_WIDTH), BF16),
        jax.ShapeDtypeStruct((n, A_WIDTH), F32),
        jax.ShapeDtypeStruct((n, MQZ_WIDTH), BF16),
        jax.ShapeDtypeStruct((n, LANES), F32),
        jax.ShapeDtypeStruct((groups, 8, tg), F32),
    )
    row = lambda w: pl.BlockSpec((tm, w), lambda i: (i, 0))
    return pl.pallas_call(
        _inproj_a_kernel,
        out_shape=outs,
        grid=(n // tm,),
        in_specs=[
            row(D_MODEL),
            pl.BlockSpec((1, D_MODEL), lambda i: (0, 0)),
            _resident((None, D_MODEL, wcols), lambda i: (l, 0, 0)),
            _resident((None, A_WIDTH, D_MODEL), lambda i: (l, 0, 0)),
            _resident((None, 8, D_MODEL), lambda i: (l, 0, 0)),
            pl.BlockSpec((None, 1, LANES), lambda i: (l, 0, 0)),
            pl.BlockSpec((None, 8, 1), lambda i: (l, 0, 0)),
        ],
        out_specs=(
            row(A_WIDTH),
            pl.BlockSpec((None, A_WIDTH, tm), lambda i: (i // per, 0, i % per)),
            row(A_WIDTH),
            row(A_WIDTH),
            row(MQZ_WIDTH),
            row(LANES),
            pl.BlockSpec((None, 8, tm), lambda i: (i // per, 0, i % per)),
        ),
        compiler_params=_params("arbitrary"),
        name="inproj_a",
    )(x, g, wrow, wkt, wgt, brow, bcol)


def _inproj_b_kernel(x_ref, g_ref, w_ref, qz_ref, mq_ref):
    xb = _rms(x_ref[...], g_ref[...]).astype(BF16)
    y = _dot(xb, w_ref[...])
    qz_ref[...] = (y[:, :QZ_WIDTH] * (DIFF_HEAD_DIM ** -0.5 * LOG2E)).astype(BF16)
    mq_ref[...] = (y[:, QZ_WIDTH:] * (MEM_HEAD_DIM ** -0.5 * LOG2E)).astype(BF16)


def _inproj_b(x, g, w, lb):
    n = x.shape[0]
    tm = min(TOKEN_TILE, n)
    return pl.pallas_call(
        _inproj_b_kernel,
        out_shape=(jax.ShapeDtypeStruct((n, QZ_WIDTH), BF16), jax.ShapeDtypeStruct((n, MQZ_WIDTH), BF16)),
        grid=(n // tm,),
        in_specs=[
            pl.BlockSpec((tm, D_MODEL), lambda i: (i, 0)),
            pl.BlockSpec((1, D_MODEL), lambda i: (0, 0)),
            _resident((None, D_MODEL, QZ_WIDTH + MQZ_WIDTH), lambda i: (lb, 0, 0)),
        ],
        out_specs=(pl.BlockSpec((tm, QZ_WIDTH), lambda i: (i, 0)), pl.BlockSpec((tm, MQZ_WIDTH), lambda i: (i, 0))),
        compiler_params=_params("arbitrary"),
        name="inproj_b",
    )(x, g, w)


def _kvproj_kernel(x_ref, g_ref, w_ref, k_ref, v_ref, kb_ref, vb_ref):
    xb = _rms(x_ref[...], g_ref[...]).astype(BF16)
    y = _dot(xb, w_ref[...])
    k = y[:, :MAIN_WIDTH]
    v = y[:, MAIN_WIDTH:]
    k_ref[...] = k
    v_ref[...] = v
    kb_ref[...] = k.astype(BF16)
    vb_ref[...] = v.astype(BF16)


def _kvproj(x, g, w):
    n = x.shape[0]
    tm = min(TOKEN_TILE, n)
    blk = pl.BlockSpec((tm, MAIN_WIDTH), lambda i: (i, 0))
    return pl.pallas_call(
        _kvproj_kernel,
        out_shape=(jax.ShapeDtypeStruct((n, MAIN_WIDTH), F32), jax.ShapeDtypeStruct((n, MAIN_WIDTH), F32),
                   jax.ShapeDtypeStruct((n, MAIN_WIDTH), BF16), jax.ShapeDtypeStruct((n, MAIN_WIDTH), BF16)),
        grid=(n // tm,),
        in_specs=[
            pl.BlockSpec((tm, D_MODEL), lambda i: (i, 0)),
            pl.BlockSpec((1, D_MODEL), lambda i: (0, 0)),
            _resident((D_MODEL, 2 * MAIN_WIDTH), lambda i: (0, 0)),
        ],
        out_specs=(blk, blk, blk, blk),
        compiler_params=_params("arbitrary"),
        name="kvproj",
    )(x, g, w)


def _kvproj_seq_kernel(x_ref, g_ref, w_ref, wkt_ref, kt_ref, v4_ref, kb_ref, vb_ref):
    xb = _rms(x_ref[...], g_ref[...]).astype(BF16)
    y = _dot(xb, w_ref[...])
    v = y[:, MAIN_WIDTH:]
    kb_ref[...] = y[:, :MAIN_WIDTH].astype(BF16)
    vb_ref[...] = v.astype(BF16)
    kt_ref[...] = _dot_nt(wkt_ref[...], xb)
    for h in range(DIFF_HEADS):
        v4_ref[h] = v[:, h * DIFF_V_DIM:(h + 1) * DIFF_V_DIM]


def _kvproj_seq(x, g, w, wkt, groups):
    n = x.shape[0]
    tg = n // groups
    tm = min(TOKEN_TILE, tg)
    per = tg // tm
    blk = pl.BlockSpec((tm, MAIN_WIDTH), lambda i: (i, 0))
    return pl.pallas_call(
        _kvproj_seq_kernel,
        out_shape=(jax.ShapeDtypeStruct((groups, MAIN_WIDTH, tg), F32),
                   jax.ShapeDtypeStruct((groups, DIFF_HEADS, tg, DIFF_V_DIM), F32),
                   jax.ShapeDtypeStruct((n, MAIN_WIDTH), BF16), jax.ShapeDtypeStruct((n, MAIN_WIDTH), BF16)),
        grid=(n // tm,),
        in_specs=[
            pl.BlockSpec((tm, D_MODEL), lambda i: (i, 0)),
            pl.BlockSpec((1, D_MODEL), lambda i: (0, 0)),
            _resident((D_MODEL, 2 * MAIN_WIDTH), lambda i: (0, 0)),
            _resident((MAIN_WIDTH, D_MODEL), lambda i: (0, 0)),
        ],
        out_specs=(pl.BlockSpec((None, MAIN_WIDTH, tm), lambda i: (i // per, 0, i % per)),
                   pl.BlockSpec((None, DIFF_HEADS, tm, DIFF_V_DIM), lambda i: (i // per, 0, i % per, 0)),
                   blk, blk),
        compiler_params=_params("arbitrary"),
        name="kvproj_seq",
    )(x, g, w, wkt)


def _memkv_kernel(m_ref, wt_ref, kt_ref, vt_ref):
    yt = _dot_nt(wt_ref[...], m_ref[...].astype(BF16))
    kt_ref[...] = yt[:MEM_WIDTH]
    vt_ref[...] = yt[MEM_WIDTH:]


def _memkv(mem, wt):
    b = mem.shape[0]
    out = jax.ShapeDtypeStruct((DEPTH, b, MEM_WIDTH, N_MEM), F32)
    blk = pl.BlockSpec((None, None, MEM_WIDTH, N_MEM), lambda l, i: (l, i, 0, 0))
    return pl.pallas_call(
        _memkv_kernel,
        out_shape=(out, out),
        grid=(DEPTH, b),
        in_specs=[pl.BlockSpec((None, N_MEM, D_MODEL), lambda l, i: (i, 0, 0)),
                  pl.BlockSpec((None, 2 * MEM_WIDTH, D_MODEL), lambda l, i: (l, 0, 0))],
        out_specs=(blk, blk),
        compiler_params=_params("arbitrary", "arbitrary"),
        name="memkv",
    )(mem, wt)


def _split3(x):
    hi = x.astype(BF16)
    r = x - hi.astype(F32)
    mid = r.astype(BF16)
    lo = (r - mid.astype(F32)).astype(BF16)
    return hi, mid, lo


def _mlstm_kernel(q_ref, kt_ref, v_ref, o_ref, gcol_ref, grow_ref, gh_ref, c0_ref, n0_ref, m0_ref,
                  h_ref, cn_ref, nn_ref, mn_ref, ct_s, m_s, tri_s, pad_s, *, sequential, chunk):
    t = pl.program_id(1)
    nb, tb = q_ref.shape[0], q_ref.shape[1]
    win = max(chunk, LANES)
    units = [(b, c * chunk) for b in range(nb) for c in range(tb // chunk)]

    @pl.when((pl.program_id(0) == 0) & (t == 0))
    def _init_tri():
        r = lax.broadcasted_iota(jnp.int32, (tb, tb), 0)
        c = lax.broadcasted_iota(jnp.int32, (tb, tb), 1)
        tri_s[...] = jnp.where((r // chunk == c // chunk) & (c <= r), 1.0, 0.0).astype(BF16)

    hd = MLSTM_HEAD_DIM

    @pl.when((t == 0) | (not sequential))
    def _init_state():
        pad_s[...] = jnp.zeros_like(pad_s)
        for slot in range(len(units)):
            for h in range(MLSTM_HEADS):
                pad_s[0:hd, 0:hd] = c0_ref[slot, h]
                pad_s[N_COL:N_COL + 1, 0:hd] = n0_ref[slot, h:h + 1, :]
                ct_s[slot, h] = pad_s[...].T
        m_s[...] = m0_ref[...]

    tri = tri_s[...]
    bcols = [sum(_dot(tri, part) for part in _split3(gcol_ref[b])) for b in range(nb)]
    brows = [sum(_dot_nt(part, tri) for part in _split3(grow_ref[b])) for b in range(nb)]

    row_i = lax.broadcasted_iota(jnp.int32, (chunk, win), 0)
    lane_i = lax.broadcasted_iota(jnp.int32, (chunk, win), 1)
    lane_row = lax.broadcasted_iota(jnp.int32, (1, win), 1)
    pad_lane = lax.broadcasted_iota(jnp.int32, (chunk, HEAD_PAD), 1)

    pairs = []
    for slot, (b, r0) in enumerate(units):
        p0 = r0 // win * win
        off = r0 - p0
        in_chunk = (lane_i >= off) & (lane_i < off + chunk)
        vis = in_chunk & (lane_i - off <= row_i)
        in_chunk_row = (lane_row >= off) & (lane_row < off + chunk)
        for h in range(MLSTM_HEADS):
            pairs.append(dict(slot=slot, b=b, r0=r0, p0=p0, off=off, h=h, vis=vis, in_chunk_row=in_chunk_row,
                              hs=slice(h * HEAD_PAD, (h + 1) * HEAD_PAD)))

    for p in pairs:
        fl = MLSTM_HEADS + p["h"]
        bcol = bcols[p["b"]][p["r0"]:p["r0"] + chunk, fl:fl + 1]
        p["brow"] = brows[p["b"]][fl:fl + 1, p["p0"]:p["p0"] + win]
        p["igrow"] = grow_ref[p["b"], p["h"]:p["h"] + 1, p["p0"]:p["p0"] + win]
        p["m_prev"] = m_s[p["slot"], p["h"]:p["h"] + 1, 0:1]
        dlog = jnp.where(p["vis"], bcol - p["brow"] + p["igrow"], -jnp.inf)
        a = jnp.max(dlog, axis=1, keepdims=True)
        p["mt"] = jnp.maximum(bcol + p["m_prev"], a)
        p["e"] = jnp.exp(dlog - p["mt"])
        p["w_inter"] = jnp.exp(bcol + p["m_prev"] - p["mt"])

    for p in pairs:
        b, r0, p0, hs = p["b"], p["r0"], p["p0"], p["hs"]
        qh = q_ref[b, r0:r0 + chunk, hs]
        p["kth"] = kt_ref[b, hs, p0:p0 + win]
        p["vh"] = v_ref[b, p0:p0 + win, hs]
        s = (_dot(qh, p["kth"]) * p["e"]).astype(BF16)
        p["ct"] = ct_s[p["slot"], p["h"]]
        p["num"] = p["w_inter"] * _dot(qh, p["ct"].astype(BF16)) + _dot(s, p["vh"])

    for p in pairs:
        b, r0, hs, num = p["b"], p["r0"], p["hs"], p["num"]
        nq = num[:, N_COL:N_COL + 1]
        den = jnp.maximum(jnp.abs(nq), jnp.exp(-p["mt"]))
        hh = jnp.where(pad_lane < MLSTM_HEAD_DIM, num / den, 0.0)
        ms = jnp.sum(hh * hh, axis=1, keepdims=True) * (1.0 / MLSTM_HEAD_DIM)
        hn = hh * lax.rsqrt(ms + EPS) * gh_ref[:, hs]
        h_ref[b, r0:r0 + chunk, hs] = (hn * jax.nn.sigmoid(o_ref[b, r0:r0 + chunk, hs])).astype(BF16)

    for p in pairs:
        last = p["off"] + chunk - 1
        b_last = p["brow"][:, last:last + 1]
        m_new = p["mt"][chunk - 1:chunk, :]
        w_end = jnp.where(p["in_chunk_row"], jnp.exp(b_last - p["brow"] + p["igrow"] - m_new), 0.0)
        decay = jnp.exp(b_last + p["m_prev"] - m_new)
        kw = (p["kth"].astype(F32) * w_end).astype(BF16)
        ct_s[p["slot"], p["h"]] = decay * p["ct"] + _dot(kw, p["vh"])
        m_s[p["slot"], p["h"]:p["h"] + 1, :] = jnp.broadcast_to(m_new, (1, LANES))

    @pl.when((t == pl.num_programs(1) - 1) | (not sequential))
    def _fin():
        for slot in range(len(units)):
            for h in range(MLSTM_HEADS):
                c_pad = ct_s[slot, h].T
                cn_ref[slot, h] = c_pad[0:hd, 0:hd]
                nn_ref[slot, h:h + 1, :] = c_pad[N_COL:N_COL + 1, 0:hd]
        mn_ref[...] = m_s[...]


def _mlstm(q, kt, v, o, gcol, grow, ghead, c0, n0, m0, l, groups, sequential):
    n = q.shape[0]
    tg = n // groups
    if sequential:
        chunk = tb = min(MLSTM_SEQ_CHUNK, tg)
        nb = min(MLSTM_GROUPS_PER_STEP, groups)
    else:
        chunk, tb, nb = CHUNK, min(MLSTM_TILE, tg), 1
    steps = tg // tb
    slots = nb * (tb // chunk)
    streams = c0.shape[1]
    hd = MLSTM_HEAD_DIM
    sblk = (lambda g, t: g) if sequential else (lambda g, t: g * steps + t)
    rows3 = lambda a: a.reshape(groups, tg, a.shape[-1])
    row = lambda w: pl.BlockSpec((nb, tb, w), lambda g, t: (g, t, 0))
    col = lambda r: pl.BlockSpec((nb, r, tb), lambda g, t: (g, 0, t))
    kernel = functools.partial(_mlstm_kernel, sequential=sequential, chunk=chunk)
    h, cn, nn, mn = pl.pallas_call(
        kernel,
        out_shape=(jax.ShapeDtypeStruct((groups, tg, A_WIDTH), BF16),
                   jax.ShapeDtypeStruct((streams, MLSTM_HEADS, hd, hd), F32),
                   jax.ShapeDtypeStruct((streams, MLSTM_HEADS, hd), F32),
                   jax.ShapeDtypeStruct((streams, 8, LANES), F32)),
        grid=(groups // nb, steps),
        in_specs=[
            row(A_WIDTH),
            col(A_WIDTH),
            row(A_WIDTH),
            row(A_WIDTH),
            row(LANES),
            col(8),
            pl.BlockSpec((None, 1, A_WIDTH), lambda g, t: (l, 0, 0)),
            pl.BlockSpec((None, slots, MLSTM_HEADS, hd, hd), lambda g, t: (l, sblk(g, t), 0, 0, 0)),
            pl.BlockSpec((None, slots, MLSTM_HEADS, hd), lambda g, t: (l, sblk(g, t), 0, 0)),
            pl.BlockSpec((None, slots, 8, LANES), lambda g, t: (l, sblk(g, t), 0, 0)),
        ],
        out_specs=(
            row(A_WIDTH),
            pl.BlockSpec((slots, MLSTM_HEADS, hd, hd), lambda g, t: (sblk(g, t), 0, 0, 0)),
            pl.BlockSpec((slots, MLSTM_HEADS, hd), lambda g, t: (sblk(g, t), 0, 0)),
            pl.BlockSpec((slots, 8, LANES), lambda g, t: (sblk(g, t), 0, 0)),
        ),
        scratch_shapes=[
            pltpu.VMEM((slots, MLSTM_HEADS, HEAD_PAD, HEAD_PAD), F32),
            pltpu.VMEM((slots, 8, LANES), F32),
            pltpu.VMEM((tb, tb), BF16),
            pltpu.VMEM((HEAD_PAD, HEAD_PAD), F32),
        ],
        compiler_params=_params("arbitrary", "arbitrary"),
        name="mlstm",
    )(rows3(q), kt, rows3(v), rows3(o), rows3(gcol), grow, ghead, c0, n0, m0)
    return h.reshape(n, A_WIDTH), cn, nn, mn


_SLOPES2 = tuple(LOG2E * 2.0 ** (-8.0 * (h + 1) / DIFF_HEADS) for h in range(DIFF_HEADS))


def _lam(lam_ref, lam_init):
    lp = lam_ref[...]
    s1 = jnp.sum(lp[0:1] * lp[1:2], axis=1, keepdims=True)
    s2 = jnp.sum(lp[2:3] * lp[3:4], axis=1, keepdims=True)
    return jnp.exp(s1) - jnp.exp(s2) + lam_init


def _vext(v_tile):
    return jnp.concatenate([v_tile, jnp.ones_like(v_tile)], axis=1)


def _finish_head(acc1, acc2, lam, gsub, lam_init):
    o = acc1[:, :DIFF_V_DIM] / acc1[:, DIFF_V_DIM:] - lam * (acc2[:, :DIFF_V_DIM] / acc2[:, DIFF_V_DIM:])
    return (_rms(o, gsub) * (1.0 - lam_init)).astype(BF16)


def _attn_prompt_kernel(it_ref, jt_ref, qz_ref, k_ref, v_ref, lam_ref, gsub_ref, o_ref, acc_s, m_s, *, lam_init, tq):
    step_id = pl.program_id(1)
    i = it_ref[step_id]
    j = jt_ref[step_id]
    tk = 2 * tq
    last = j == (i >> 1)
    odd = (i & 1) == 1

    @pl.when(j == 0)
    def _init():
        acc_s[...] = jnp.zeros_like(acc_s)
        m_s[...] = jnp.full_like(m_s, -jnp.inf)

    def step(keys, bias_fn):
        def scores(idx):
            h = idx % DIFF_HEADS
            qt = qz_ref[:, idx * LANES:(idx + 1) * LANES]
            kt_i = idx // DIFF_HEADS * (DIFF_HEADS // 2) + h // 2
            kt = k_ref[0:keys, kt_i * LANES:(kt_i + 1) * LANES]
            return bias_fn(_dot_nt(qt, kt), h)

        def values(idx, p, alpha):
            vt = _vext(v_ref[0:keys, idx % DIFF_HEADS * LANES:(idx % DIFF_HEADS + 1) * LANES])
            acc_s[idx] = jnp.concatenate([alpha, alpha], axis=1) * acc_s[idx] + _dot(p, vt)

        s_next = scores(0)
        pending = None
        for idx in range(2 * DIFF_HEADS):
            s = s_next
            if idx + 1 < 2 * DIFF_HEADS:
                s_next = scores(idx + 1)
            if pending is not None:
                values(*pending)
            m_prev = m_s[idx]
            m_new = jnp.maximum(m_prev, jnp.max(s, axis=1, keepdims=True))
            p = jnp.exp2(s - jnp.concatenate([m_new] * (keys // LANES), axis=1)).astype(BF16)
            pending = (idx, p, jnp.exp2(m_prev - m_new))
            m_s[idx] = m_new
        values(*pending)

    def finish():
        lam = _lam(lam_ref, lam_init)
        for h in range(DIFF_HEADS):
            o_ref[:, h * LANES:(h + 1) * LANES] = _finish_head(
                acc_s[h], acc_s[DIFF_HEADS + h], lam, gsub_ref[...], lam_init)

    def own_keys(keys):
        r = lax.broadcasted_iota(jnp.int32, (tq, keys), 0)
        c = lax.broadcasted_iota(jnp.int32, (tq, keys), 1) - (keys - tq)
        vis = (c < 0) | (jnp.maximum(c, 0) // CHUNK <= r // CHUNK)
        g = (r - jnp.abs(r - c)).astype(F32)
        step(keys, lambda s, h: jnp.where(vis, s + _SLOPES2[h] * g, -jnp.inf))
        finish()

    @pl.when(jnp.logical_not(last))
    def _past():
        col = lax.broadcasted_iota(jnp.int32, (1, tk), 1).astype(F32)
        off = (j * tk - i * tq).astype(F32)
        step(tk, lambda s, h: s + _SLOPES2[h] * (col + off))

    @pl.when(last & odd)
    def _diag_odd():
        own_keys(tk)

    @pl.when(last & jnp.logical_not(odd))
    def _diag_even():
        own_keys(tq)


def _attn_prompt(qz, kb, vb, lambdas, gsub, lam_init, groups):
    n = qz.shape[0]
    tg = n // groups
    tq = min(ATTN_TQ, tg // 2)
    tk = 2 * tq
    nq = tg // tq
    nkv = tg // tk
    steps = [(i, j) for i in range(nq) for j in range(i // 2 + 1)]
    itab = jnp.asarray([s[0] for s in steps], jnp.int32)
    jtab = jnp.asarray([s[1] for s in steps], jnp.int32)
    kernel = functools.partial(_attn_prompt_kernel, lam_init=lam_init, tq=tq)
    q_spec = lambda w: pl.BlockSpec((tq, w), lambda b, s, it, jt: (b * nq + it[s], 0))
    kv_spec = pl.BlockSpec((tk, MAIN_WIDTH), lambda b, s, it, jt: (b * nkv + jt[s], 0))
    grid_spec = pltpu.PrefetchScalarGridSpec(
        num_scalar_prefetch=2,
        grid=(groups, len(steps)),
        in_specs=[
            q_spec(QZ_WIDTH),
            kv_spec,
            kv_spec,
            pl.BlockSpec((4, DIFF_HEAD_DIM), lambda b, s, it, jt: (0, 0)),
            pl.BlockSpec((1, DIFF_V_DIM), lambda b, s, it, jt: (0, 0)),
        ],
        out_specs=q_spec(MAIN_WIDTH),
        scratch_shapes=[
            pltpu.VMEM((2 * DIFF_HEADS, tq, 2 * DIFF_V_DIM), F32),
            pltpu.VMEM((2 * DIFF_HEADS, tq, LANES), F32),
        ],
    )
    return pl.pallas_call(
        kernel,
        out_shape=jax.ShapeDtypeStruct((n, MAIN_WIDTH), BF16),
        grid_spec=grid_spec,
        compiler_params=_params("arbitrary", "arbitrary"),
        name="attn_prompt",
    )(itab, jtab, qz, kb, vb, lambdas, gsub)


def _attn_sample_kernel(qz_ref, ckt_ref, cv_ref, nk_ref, nv_ref, lam_ref, gsub_ref, o_ref, *, lam_init, past):
    tq = qz_ref.shape[0]

    def geometry(width, c0):
        r = lax.broadcasted_iota(jnp.int32, (tq, width), 0) + past
        c = lax.broadcasted_iota(jnp.int32, (tq, width), 1) + c0
        return c // CHUNK <= r // CHUNK, jnp.abs(r - c).astype(F32)

    vis_c, dist_c = geometry(past, 0)
    vis_n, dist_n = geometry(tq, past)
    lam = _lam(lam_ref, lam_init)
    probs = {}
    for kt_i in range(DIFF_HEADS):
        a, h0 = kt_i // (DIFF_HEADS // 2), 2 * (kt_i % (DIFF_HEADS // 2))
        ks = slice(kt_i * LANES, (kt_i + 1) * LANES)
        q2 = jnp.concatenate([qz_ref[:, (a * DIFF_HEADS + h) * LANES:(a * DIFF_HEADS + h + 1) * LANES]
                              for h in (h0, h0 + 1)], axis=0)
        sc2 = _dot(q2, ckt_ref[ks, :].astype(BF16))
        sn2 = _dot_nt(q2, nk_ref[:, ks])
        for r, h in enumerate((h0, h0 + 1)):
            rows = slice(r * tq, (r + 1) * tq)
            s_c = jnp.where(vis_c, sc2[rows] - _SLOPES2[h] * dist_c, -jnp.inf)
            s_n = jnp.where(vis_n, sn2[rows] - _SLOPES2[h] * dist_n, -jnp.inf)
            m = jnp.maximum(jnp.max(s_c, axis=1, keepdims=True), jnp.max(s_n, axis=1, keepdims=True))
            probs[a, h] = (jnp.exp2(s_c - m).astype(BF16), jnp.exp2(s_n - m).astype(BF16))
    for h in range(DIFF_HEADS):
        p_c = jnp.concatenate([probs[0, h][0], probs[1, h][0]], axis=0)
        p_n = jnp.concatenate([probs[0, h][1], probs[1, h][1]], axis=0)
        acc = (_dot(p_c, _vext(cv_ref[h].astype(BF16)))
               + _dot(p_n, _vext(nv_ref[:, h * LANES:(h + 1) * LANES])))
        o_ref[:, h * LANES:(h + 1) * LANES] = _finish_head(acc[0:tq], acc[tq:2 * tq], lam, gsub_ref[...], lam_init)


def _attn_sample(qz, cache_kt, cache_v, kb, vb, lambdas, gsub, lam_init):
    streams, _, past = cache_kt.shape
    n = qz.shape[0]
    tq = n // streams
    kernel = functools.partial(_attn_sample_kernel, lam_init=lam_init, past=past)
    new = pl.BlockSpec((tq, MAIN_WIDTH), lambda b: (b, 0))
    return pl.pallas_call(
        kernel,
        out_shape=jax.ShapeDtypeStruct((n, MAIN_WIDTH), BF16),
        grid=(streams,),
        in_specs=[
            pl.BlockSpec((tq, QZ_WIDTH), lambda b: (b, 0)),
            pl.BlockSpec((None, MAIN_WIDTH, past), lambda b: (b, 0, 0)),
            pl.BlockSpec((None, DIFF_HEADS, past, DIFF_V_DIM), lambda b: (b, 0, 0, 0)),
            new, new,
            pl.BlockSpec((4, DIFF_HEAD_DIM), lambda b: (0, 0)),
            pl.BlockSpec((1, DIFF_V_DIM), lambda b: (0, 0)),
        ],
        out_specs=new,
        compiler_params=_params("arbitrary"),
        name="attn_sample",
    )(qz, cache_kt, cache_v, kb, vb, lambdas, gsub)


def _memattn_kernel(mq_ref, mkt_ref, mvt_ref, o_ref, *, tq):
    lane = lax.broadcasted_iota(jnp.int32, (tq, LANES), 1)
    for i in range(mkt_ref.shape[0]):
        rows = slice(i * tq, (i + 1) * tq)
        for pair in range(MEM_HEADS // 2):
            ks = slice(pair * LANES, (pair + 1) * LANES)
            mkt = mkt_ref[i, ks, :].astype(BF16)
            mvt = mvt_ref[i, ks, :].astype(BF16)
            outs = []
            for h in (2 * pair, 2 * pair + 1):
                s = _dot(mq_ref[rows, h * LANES:(h + 1) * LANES], mkt)
                p = jnp.exp2(s - jnp.max(s, axis=1, keepdims=True))
                l = jnp.sum(p, axis=1, keepdims=True)
                outs.append(_dot_nt(p.astype(BF16), mvt) / l)
            o_ref[rows, ks] = jnp.where(lane < MEM_HEAD_DIM, outs[0], outs[1]).astype(BF16)


def _memattn(mq, mkt, mvt, l):
    n = mq.shape[0]
    streams = mkt.shape[1]
    tg = n // streams
    tq = min(MEM_TILE, tg)
    per = tg // tq
    ns = MEM_TILE // tq
    rows = ns * tq
    mem = pl.BlockSpec((None, ns, MEM_WIDTH, N_MEM), lambda i: (l, i // per, 0, 0))
    return pl.pallas_call(
        functools.partial(_memattn_kernel, tq=tq),
        out_shape=jax.ShapeDtypeStruct((n, MEM_WIDTH), BF16),
        grid=(n // rows,),
        in_specs=[pl.BlockSpec((rows, MQZ_WIDTH), lambda i: (i, 0)), mem, mem],
        out_specs=pl.BlockSpec((rows, MEM_WIDTH), lambda i: (i, 0)),
        compiler_params=_params("arbitrary"),
        name="memattn",
    )(mq, mkt, mvt)


def _pad_heads(w, heads, dim, pad):
    lead = w.shape[:-1]
    w = w.reshape(lead + (heads, dim))
    w = jnp.pad(w, [(0, 0)] * len(lead) + [(0, 0), (0, pad - dim)])
    return w.reshape(lead + (heads * pad,))


def _interleave64(w, heads):
    lead = w.shape[:-1]
    w = w.reshape(lead + (heads, 64))
    z = jnp.zeros_like(w)
    even = jnp.concatenate([w, z], axis=-1)
    odd = jnp.concatenate([z, w], axis=-1)
    sel = (jnp.arange(heads) % 2 == 0)[:, None]
    return jnp.where(sel, even, odd).reshape(lead + (heads * LANES,))


def _prep_weights(w_up, w_down, norm_g, w_in_a, b_gates, g_head_a, w_in_b, w_kv, w_mem_kv, w_out):
    mw = MAIN_WIDTH
    prep = {
        "w_up": w_up.astype(BF16),
        "w_down": w_down.astype(BF16),
        "norm_g": norm_g.reshape(DEPTH, 6, 1, D_MODEL),
        "w_kv": w_kv.astype(BF16),
        "w_kt": w_kv[:, :mw].T.astype(BF16),
        "w_mem_kv_t": jnp.swapaxes(w_mem_kv, 1, 2).astype(BF16),
        "w_out_mem": w_out[:, mw:, :].astype(BF16),
        "w_out_b": w_out[N_A_LAYERS:, :mw, :].astype(BF16),
    }
    wa = w_out[:N_A_LAYERS, :mw, :].reshape(N_A_LAYERS, MLSTM_HEADS, MLSTM_HEAD_DIM, D_MODEL)
    wa = jnp.pad(wa, ((0, 0), (0, 0), (0, HEAD_PAD - MLSTM_HEAD_DIM), (0, 0)))
    prep["w_out_a"] = wa.reshape(N_A_LAYERS, A_WIDTH, D_MODEL).astype(BF16)
    pad_a = lambda w: _pad_heads(w, MLSTM_HEADS, MLSTM_HEAD_DIM, HEAD_PAD)
    q, k, v, o = (pad_a(w_in_a[:, :, s * mw:(s + 1) * mw]) for s in range(4))
    gates = w_in_a[:, :, 4 * mw:4 * mw + 2 * MLSTM_HEADS]
    mq = _interleave64(w_in_a[:, :, 4 * mw + 2 * MLSTM_HEADS:], MEM_HEADS)
    gates_pad = jnp.pad(gates, ((0, 0), (0, 0), (0, LANES - 2 * MLSTM_HEADS)))
    prep["wrow_a"] = jnp.concatenate([q, v, o, mq, gates_pad], axis=-1).astype(BF16)
    prep["wkt_a"] = jnp.swapaxes(k, 1, 2).astype(BF16)
    prep["wgt_a"] = jnp.swapaxes(gates, 1, 2).astype(BF16)
    prep["brow_a"] = jnp.pad(b_gates, ((0, 0), (0, LANES - 2 * MLSTM_HEADS))).reshape(N_A_LAYERS, 1, LANES)
    prep["bcol_a"] = b_gates.reshape(N_A_LAYERS, 2 * MLSTM_HEADS, 1)
    prep["ghead_a"] = jnp.pad(g_head_a, ((0, 0), (0, 0), (0, HEAD_PAD - MLSTM_HEAD_DIM))).reshape(N_A_LAYERS, 1, A_WIDTH)
    nq = 2 * DIFF_HEADS * DIFF_HEAD_DIM
    prep["w_in_b"] = jnp.concatenate(
        [_interleave64(w_in_b[:, :, :nq], 2 * DIFF_HEADS), _interleave64(w_in_b[:, :, nq:], MEM_HEADS)],
        axis=-1).astype(BF16)
    return prep


def _trunk(x, groups, mem_k, mem_v, c0, n0, m0, cache_k, cache_v, W, lambdas, g_sub_b, g_kv):
    prompt = cache_k is None
    g = W["norm_g"]
    states = []
    k_new = v_new = kb = vb = None
    for l in range(DEPTH):
        x = _ffn(x, g[l, 0], W["w_up"], W["w_down"], g[l, 1], l, 0)
        if l < N_A_LAYERS:
            q, kt, v, o, mq, gcol, grow = _inproj_a(
                x, g[l, 2], W["wrow_a"], W["wkt_a"], W["wgt_a"], W["brow_a"], W["bcol_a"], l, groups)
            main, cn, nn, mn = _mlstm(q, kt, v, o, gcol, grow, W["ghead_a"], c0, n0, m0, l, groups, prompt)
            states.append((cn, nn, mn))
            w_main, lm = W["w_out_a"], l
        else:
            lb = l - N_A_LAYERS
            lam_init = 0.8 - 0.6 * math.exp(-0.3 * l)
            qz, mq = _inproj_b(x, g[l, 2], W["w_in_b"], lb)
            if prompt:
                main = _attn_prompt(qz, kb, vb, lambdas[lb], g_sub_b[lb], lam_init, groups)
            else:
                main = _attn_sample(qz, cache_k, cache_v, kb, vb, lambdas[lb], g_sub_b[lb], lam_init)
            w_main, lm = W["w_out_b"], lb
        mem = _memattn(mq, mem_k, mem_v, l)
        x = _mix_ffn(main, mem, x, w_main, lm, W["w_out_mem"], g[l, 3], g[l, 4], W["w_up"], W["w_down"], g[l, 5], l)
        if l == N_A_LAYERS - 1:
            if prompt:
                k_new, v_new, kb, vb = _kvproj_seq(x, g_kv, W["w_kv"], W["w_kt"], groups)
            else:
                k_new, v_new, kb, vb = _kvproj(x, g_kv, W["w_kv"])
    return x, states, k_new, v_new


def _unpack_states(states):
    c, n, m = (jnp.stack([s[i] for s in states]) for i in range(3))
    return c, n, m[..., :MLSTM_HEADS, 0]


def kernel(x_prompt, x_sample, mem_prompt, cache_mem_k, cache_mem_v, state_C, state_n, state_m, cache_k, cache_v,
           w_up, w_down, norm_g, w_in_a, b_gates, g_head_a, w_in_b, lambdas, g_sub_b, g_kv, w_kv, w_mem_kv, w_out):
    W = _prep_weights(w_up, w_down, norm_g, w_in_a, b_gates, g_head_a, w_in_b, w_kv, w_mem_kv, w_out)
    g_kv2 = g_kv.reshape(1, D_MODEL)
    g_sub = g_sub_b.reshape(-1, 1, DIFF_V_DIM)
    hd = MLSTM_HEAD_DIM

    bp, tp, _ = x_prompt.shape
    mkt_p, mvt_p = _memkv(mem_prompt, W["w_mem_kv_t"])
    c0 = jnp.zeros((N_A_LAYERS, bp, MLSTM_HEADS, hd, hd), F32)
    n0 = jnp.zeros((N_A_LAYERS, bp, MLSTM_HEADS, hd), F32)
    m0 = jnp.zeros((N_A_LAYERS, bp, 8, LANES), F32)
    y_p, st_p, kt_p, v4_p = _trunk(x_prompt.reshape(bp * tp, D_MODEL), bp, mkt_p, mvt_p, c0, n0, m0, None, None,
                                   W, lambdas, g_sub, g_kv2)
    c_p, n_p, m_p = _unpack_states(st_p)

    bs, ts, _ = x_sample.shape
    m_s = jnp.broadcast_to(jnp.pad(state_m, ((0, 0), (0, 0), (0, 8 - MLSTM_HEADS)))[..., None],
                           (N_A_LAYERS, bs, 8, LANES))
    mem_t = lambda a: jnp.transpose(a, (0, 1, 3, 4, 2)).reshape(DEPTH, -1, MEM_WIDTH, N_MEM)
    y_s, st_s, k_s, v_s = _trunk(
        x_sample.reshape(bs * ts, D_MODEL), 1, mem_t(cache_mem_k), mem_t(cache_mem_v), state_C, state_n, m_s,
        jnp.transpose(cache_k, (0, 2, 3, 1)).reshape(bs, MAIN_WIDTH, -1), jnp.transpose(cache_v, (0, 2, 1, 3)),
        W, lambdas, g_sub, g_kv2)
    c_s, n_s, m_sn = _unpack_states(st_s)

    kshape = (2 * DIFF_HEADS, DIFF_HEAD_DIM)
    vshape = (DIFF_HEADS, DIFF_V_DIM)
    mem_out = lambda a: jnp.transpose(a.reshape(DEPTH, bp, MEM_HEADS, MEM_HEAD_DIM, N_MEM), (0, 1, 4, 2, 3))
    return (y_p.reshape(bp, tp, D_MODEL), y_s.reshape(bs, ts, D_MODEL),
            mem_out(mkt_p), mem_out(mvt_p),
            c_p, n_p, m_p,
            jnp.transpose(kt_p.reshape((bp,) + kshape + (tp,)), (0, 3, 1, 2)), jnp.transpose(v4_p, (0, 2, 1, 3)),
            c_s, n_s, m_sn,
            k_s.reshape((bs, ts) + kshape), v_s.reshape((bs, ts) + vshape))
```
